```python
import math
import jax, jax.numpy as jnp
from jax import lax
import numpy as np

D_MODEL = 1024
BATCH = 8
SEQ = 4096
DEPTH = 2

N_META = 16
CHUNK = 64
Q_BLOCK = 128
MIX_WIDTH = D_MODEL
LRU_WIDTH = MIX_WIDTH // 2
LRU_BLOCKS = 8
LRU_BLOCK_DIM = LRU_WIDTH // LRU_BLOCKS
LRU_CONV = 4
LRU_C = 8.0
ATT_WIDTH = MIX_WIDTH - LRU_WIDTH
DIFF_HEADS = 4
DIFF_HEAD_DIM = ATT_WIDTH // (2 * DIFF_HEADS)
V_HEAD_DIM = 2 * DIFF_HEAD_DIM
ROPE_THETA = 10000.0
D_FF = 3 * D_MODEL
FFN_CONV = 3
NORM_EPS = 1e-6
SUBLN_EPS = 1e-5
IN_WIDTH = 2 * LRU_WIDTH + 3 * ATT_WIDTH
NEG_INF = -1e30

kernel_name = "hymba_rglru_diffattn_convffn"


def rms_norm(x, g, eps=NORM_EPS):
    xf = x.astype(jnp.float32)
    y = xf * lax.rsqrt(jnp.mean(xf * xf, axis=-1, keepdims=True) + eps)
    return (y * g.astype(jnp.float32)).astype(x.dtype)


def causal_depthwise_conv(x, w, b):
    K = w.shape[0]
    T = x.shape[1]
    xp = jnp.pad(x, ((0, 0), (K - 1, 0), (0, 0)))
    y = b
    for k in range(K):
        y = y + xp[:, k:k + T, :] * w[k]
    return y


def rope_tables(T, dim):
    inv = 1.0 / (ROPE_THETA ** (jnp.arange(0, dim, 2, dtype=jnp.float32) / dim))
    ang = jnp.arange(T, dtype=jnp.float32)[:, None] * inv[None, :]
    return jnp.cos(ang), jnp.sin(ang)


def apply_rope(x, cos, sin):
    x1, x2 = jnp.split(x, 2, axis=-1)
    c = cos.astype(x.dtype)
    s = sin.astype(x.dtype)
    return jnp.concatenate([x1 * c - x2 * s, x2 * c + x1 * s], axis=-1)


def chunk_ids(T_pad, T):
    p = jnp.arange(T_pad, dtype=jnp.int32)
    cid = jnp.where(p < N_META, 0, 1 + (p - N_META) // CHUNK)
    return jnp.where(p < T, cid, jnp.iinfo(jnp.int32).max).astype(jnp.int32)


def rg_lru(x, wa, ba, wx, bx, a_param):
    B, T, _ = x.shape
    xb = x.reshape(B, T, LRU_BLOCKS, LRU_BLOCK_DIM)
    r = jax.nn.sigmoid(jnp.einsum('btnc,ncd->btnd', xb, wa) + ba).reshape(B, T, LRU_WIDTH)
    i = jax.nn.sigmoid(jnp.einsum('btnc,ncd->btnd', xb, wx) + bx).reshape(B, T, LRU_WIDTH)
    log_a = LRU_C * r.astype(jnp.float32) * jax.nn.log_sigmoid(a_param.astype(jnp.float32))
    a = jnp.exp(log_a)
    mult = jnp.sqrt(-jnp.expm1(2.0 * log_a))
    bterm = mult * (i * x).astype(jnp.float32)

    def combine(left, right):
        a1, b1 = left
        a2, b2 = right
        return a1 * a2, a2 * b1 + b2

    _, h = lax.associative_scan(combine, (a, bterm), axis=1)
    return h.astype(x.dtype)


def diff_attention(q, k, v, lam_q1, lam_k1, lam_q2, lam_k2, subln, lambda_init):
    B, T, _ = q.shape
    T_pad = -(-T // Q_BLOCK) * Q_BLOCK
    n_blk = T_pad // Q_BLOCK
    pad = T_pad - T
    q = q.reshape(B, T, DIFF_HEADS, 2, DIFF_HEAD_DIM).transpose(3, 0, 2, 1, 4)
    k = k.reshape(B, T, DIFF_HEADS, 2, DIFF_HEAD_DIM).transpose(3, 0, 2, 1, 4)
    v = v.reshape(B, T, DIFF_HEADS, V_HEAD_DIM).transpose(0, 2, 1, 3)
    cos, sin = rope_tables(T, DIFF_HEAD_DIM)
    q = apply_rope(q, cos, sin) * (DIFF_HEAD_DIM ** -0.5)
    k = apply_rope(k, cos, sin)
    q = jnp.pad(q, ((0, 0), (0, 0), (0, 0), (0, pad), (0, 0)))
    k = jnp.pad(k, ((0, 0), (0, 0), (0, 0), (0, pad), (0, 0)))
    v = jnp.pad(v, ((0, 0), (0, 0), (0, pad), (0, 0)))
    cid = chunk_ids(T_pad, T)
    lam = (jnp.exp(jnp.sum(lam_q1.astype(jnp.float32) * lam_k1.astype(jnp.float32)))
           - jnp.exp(jnp.sum(lam_q2.astype(jnp.float32) * lam_k2.astype(jnp.float32)))
           + lambda_init)
    qb = q.reshape(2, B, DIFF_HEADS, n_blk, Q_BLOCK, DIFF_HEAD_DIM).transpose(3, 0, 1, 2, 4, 5)
    cb = cid.reshape(n_blk, Q_BLOCK)

    def block(args):
        qi, ci = args
        s = jnp.einsum('mbhqd,mbhkd->mbhqk', qi, k).astype(jnp.float32)
        mask = cid[None, :] <= ci[:, None]
        s = jnp.where(mask, s, NEG_INF)
        p = jax.nn.softmax(s, axis=-1)
        w = p[0] - lam * p[1]
        return jnp.einsum('bhqk,bhke->bhqe', w.astype(v.dtype), v)

    o = lax.map(block, (qb, cb))
    o = o.transpose(1, 2, 0, 3, 4).reshape(B, DIFF_HEADS, T_pad, V_HEAD_DIM)[:, :, :T]
    o = rms_norm(o, subln, SUBLN_EPS) * (1.0 - lambda_init)
    return o.transpose(0, 2, 1, 3).reshape(B, T, ATT_WIDTH)


def setup_inputs(seed: int = 0) -> dict:
    key = jax.random.key(seed)
    ks = jax.random.split(key, 26)
    f32 = jnp.float32
    L = DEPTH

    def nrm(k, shape, scale):
        return jax.random.normal(k, shape, f32) * scale

    def gain(k, shape):
        return 1.0 + 0.02 * jax.random.normal(k, shape, f32)

    u = jax.random.uniform(ks[13], (L, LRU_WIDTH), f32, 0.9, 0.999)
    s = u ** (1.0 / LRU_C)
    a_param = jnp.log(s) - jnp.log1p(-s)
    return {
        "x": nrm(ks[0], (BATCH, SEQ, D_MODEL), 1.0),
        "meta_tokens": nrm(ks[1], (N_META, D_MODEL), 1.0),
        "ln_mix_pre": gain(ks[2], (L, D_MODEL)),
        "ln_mix_post": gain(ks[3], (L, D_MODEL)),
        "ln_ffn_pre": gain(ks[4], (L, D_MODEL)),
        "ln_ffn_post": gain(ks[5], (L, D_MODEL)),
        "w_in": nrm(ks[6], (L, D_MODEL, IN_WIDTH), D_MODEL ** -0.5),
        "lru_conv_w": nrm(ks[7], (L, LRU_CONV, LRU_WIDTH), LRU_CONV ** -0.5),
        "lru_conv_b": nrm(ks[8], (L, LRU_WIDTH), 0.01),
        "lru_wa": nrm(ks[9], (L, LRU_BLOCKS, LRU_BLOCK_DIM, LRU_BLOCK_DIM), LRU_BLOCK_DIM ** -0.5),
        "lru_ba": nrm(ks[10], (L, LRU_BLOCKS, LRU_BLOCK_DIM), 0.01),
        "lru_wx": nrm(ks[11], (L, LRU_BLOCKS, LRU_BLOCK_DIM, LRU_BLOCK_DIM), LRU_BLOCK_DIM ** -0.5),
        "lru_bx": nrm(ks[12], (L, LRU_BLOCKS, LRU_BLOCK_DIM), 0.01),
        "lru_a_param": a_param,
        "lru_out_norm": gain(ks[14], (L, LRU_WIDTH)),
        "lam_q1": nrm(ks[15], (L, DIFF_HEAD_DIM), 0.1),
        "lam_k1": nrm(ks[16], (L, DIFF_HEAD_DIM), 0.1),
        "lam_q2": nrm(ks[17], (L, DIFF_HEAD_DIM), 0.1),
        "lam_k2": nrm(ks[18], (L, DIFF_HEAD_DIM), 0.1),
        "diff_subln": gain(ks[19], (L, V_HEAD_DIM)),
        "w_out": nrm(ks[20], (L, MIX_WIDTH, D_MODEL), MIX_WIDTH ** -0.5),
        "w_up": nrm(ks[21], (L, D_MODEL, 2 * D_FF), D_MODEL ** -0.5),
        "ffn_conv_w": nrm(ks[22], (L, FFN_CONV, D_FF), FFN_CONV ** -0.5),
        "ffn_conv_b": nrm(ks[23], (L, D_FF), 0.01),
        "w_down": nrm(ks[24], (L, D_FF, D_MODEL), D_FF ** -0.5),
    }


def reference(x, meta_tokens, ln_mix_pre, ln_mix_post, ln_ffn_pre, ln_ffn_post, w_in,
              lru_conv_w, lru_conv_b, lru_wa, lru_ba, lru_wx, lru_bx, lru_a_param,
              lru_out_norm, lam_q1, lam_k1, lam_q2, lam_k2, diff_subln, w_out,
              w_up, ffn_conv_w, ffn_conv_b, w_down):
    B = x.shape[0]
    meta = jnp.broadcast_to(meta_tokens[None].astype(x.dtype), (B, N_META, D_MODEL))
    h = jnp.concatenate([meta, x], axis=1)
    splits = [LRU_WIDTH, 2 * LRU_WIDTH, 2 * LRU_WIDTH + ATT_WIDTH, 2 * LRU_WIDTH + 2 * ATT_WIDTH]
    for l in range(DEPTH):
        lambda_init = 0.8 - 0.6 * math.exp(-0.3 * l)
        u = rms_norm(h, ln_mix_pre[l])
        z = u @ w_in[l]
        xr, gr, q, k, v = jnp.split(z, splits, axis=-1)
        xr = causal_depthwise_conv(xr, lru_conv_w[l], lru_conv_b[l])
        hr = rg_lru(xr, lru_wa[l], lru_ba[l], lru_wx[l], lru_bx[l], lru_a_param[l])
        yr = rms_norm(hr * jax.nn.gelu(gr), lru_out_norm[l])
        ya = diff_attention(q, k, v, lam_q1[l], lam_k1[l], lam_q2[l], lam_k2[l],
                            diff_subln[l], lambda_init)
        y = jnp.concatenate([yr, ya], axis=-1) @ w_out[l]
        h = h + rms_norm(y, ln_mix_post[l])
        u = rms_norm(h, ln_ffn_pre[l])
        g, val = jnp.split(u @ w_up[l], 2, axis=-1)
        g = causal_depthwise_conv(g, ffn_conv_w[l], ffn_conv_b[l])
        f = (jax.nn.gelu(g) * val) @ w_down[l]
        h = h + rms_norm(f, ln_ffn_post[l])
    return h[:, N_META:]
```

```python
import functools
import math

import jax
import jax.numpy as jnp
from jax import lax
from jax.experimental import pallas as pl
from jax.experimental.pallas import tpu as pltpu

D_MODEL = 1024
N_META = 16
CHUNK = 64
LRU_WIDTH = 512
LRU_BLOCKS = 8
LRU_CONV = 4
LRU_C = 8.0
ATT_WIDTH = 512
HEADS = 4
HEAD_DIM = 64
V_DIM = 128
ROPE_THETA = 10000.0
D_FF = 3 * D_MODEL
FFN_CONV = 3
NORM_EPS = 1e-6
SUBLN_EPS = 1e-5
IN_WIDTH = 2 * LRU_WIDTH + 3 * ATT_WIDTH
NEG_INF = -1e30

LANES = 128
SUBLANES = 8
BF16_ROWS = 16
GATE_TILE = 256

VMEM_LIMIT = 56 * 1024 * 1024

F32 = jnp.float32
BF16 = jnp.bfloat16


def _dot(a, b):
    return jnp.dot(a, b, preferred_element_type=F32)


def _gelu(x):
    c = math.sqrt(2.0 / math.pi)
    return 0.5 * x * (1.0 + jnp.tanh(c * (x + 0.044715 * (x * x * x))))


def _sigmoid(x):
    return 1.0 / (1.0 + jnp.exp(-x))


def _rms(x, eps):
    return x * lax.rsqrt(jnp.mean(x * x, axis=-1, keepdims=True) + eps)


def _params(*sem):
    return pltpu.CompilerParams(dimension_semantics=sem, vmem_limit_bytes=VMEM_LIMIT)


def _full(shape):
    n = len(shape)
    return pl.BlockSpec(shape, lambda *_: (0,) * n)


def _in_proj_kernel(h_ref, g_ref, w_ref, cos_ref, sin_ref,
                    xr_ref, gr_ref, qt_ref, k_ref, v_ref):
    u = (_rms(h_ref[...], NORM_EPS) * g_ref[...]).astype(BF16)
    xr_ref[...] = _dot(u, w_ref[:, 0:LRU_WIDTH])
    gr_ref[...] = _dot(u, w_ref[:, LRU_WIDTH:2 * LRU_WIDTH]).astype(BF16)
    o = 2 * LRU_WIDTH
    zq = _dot(u, w_ref[:, o:o + ATT_WIDTH])
    zk = _dot(u, w_ref[:, o + ATT_WIDTH:o + 2 * ATT_WIDTH])
    v_ref[...] = _dot(u, w_ref[:, o + 2 * ATT_WIDTH:o + 3 * ATT_WIDTH]).astype(BF16)

    cos = cos_ref[...]
    sin = sin_ref[...]
    tm = cos.shape[0]
    lane = lax.broadcasted_iota(jnp.int32, (tm, LANES), 1)
    low_half = (lane & (HEAD_DIM // 2)) == 0

    def rope(x):
        up = pltpu.roll(x, LANES - HEAD_DIM // 2, 1)
        down = pltpu.roll(x, HEAD_DIM // 2, 1)
        return x * cos + jnp.where(low_half, up, down) * sin

    for hd in range(HEADS):
        sl = slice(hd * LANES, (hd + 1) * LANES)
        qh = rope(zq[:, sl]) * (HEAD_DIM ** -0.5)
        qt_ref[sl, :] = qh.T.astype(BF16)
        k_ref[:, sl] = rope(zk[:, sl]).astype(BF16)


def _in_proj(h, g, w, cos, sin, tm):
    B, TP, _ = h.shape
    row = lambda width, dt: jax.ShapeDtypeStruct((B, TP, width), dt)
    rows = lambda width: pl.BlockSpec((None, tm, width), lambda b, i: (b, i, 0))
    return pl.pallas_call(
        _in_proj_kernel,
        grid=(B, TP // tm),
        in_specs=[rows(D_MODEL), _full((1, D_MODEL)), _full((D_MODEL, IN_WIDTH)),
                  pl.BlockSpec((tm, LANES), lambda b, i: (i, 0)),
                  pl.BlockSpec((tm, LANES), lambda b, i: (i, 0))],
        out_specs=[rows(LRU_WIDTH), rows(LRU_WIDTH),
                   pl.BlockSpec((None, ATT_WIDTH, tm), lambda b, i: (b, 0, i)),
                   rows(ATT_WIDTH), rows(ATT_WIDTH)],
        out_shape=[row(LRU_WIDTH, F32), row(LRU_WIDTH, BF16),
                   jax.ShapeDtypeStruct((B, ATT_WIDTH, TP), BF16),
                   row(ATT_WIDTH, BF16), row(ATT_WIDTH, BF16)],
        compiler_params=_params("parallel", "arbitrary"),
        name="in_proj",
    )(h, g, w, cos, sin)


def _lru_kernel(xr_ref, gr_ref, cw_ref, cb_ref, wa_ref, ba_ref, wx_ref, bx_ref,
                ap_ref, gn_ref, out_ref, xext, a_s, b_s, tail, carry):
    tt = xr_ref.shape[0]

    @pl.when(pl.program_id(1) == 0)
    def _():
        tail[...] = jnp.zeros_like(tail)
        carry[...] = jnp.zeros_like(carry)

    xext[0:SUBLANES, :] = tail[...]
    xext[SUBLANES:, :] = xr_ref[...]
    tail[...] = xr_ref[tt - SUBLANES:, :]

    xc = cb_ref[...]
    for k in range(LRU_CONV):
        off = SUBLANES - (LRU_CONV - 1) + k
        xc = xc + xext[off:off + tt, :] * cw_ref[k:k + 1, :]

    xcb = xc.astype(BF16)

    def gate(w_ref, b_ref):
        parts = [_dot(xcb[:, j * GATE_TILE:(j + 1) * GATE_TILE], w_ref[j])
                 for j in range(LRU_WIDTH // GATE_TILE)]
        return _sigmoid(jnp.concatenate(parts, axis=1) + b_ref[...])

    r = gate(wa_ref, ba_ref)
    gi = gate(wx_ref, bx_ref)
    ap = ap_ref[...]
    log_sig = jnp.minimum(ap, 0.0) - jnp.log1p(jnp.exp(-jnp.abs(ap)))
    a = jnp.exp(LRU_C * r * log_sig)
    a_s[...] = a
    b_s[...] = jnp.sqrt(1.0 - a * a) * (gi * xc)

    row = lax.broadcasted_iota(jnp.int32, (SUBLANES, LRU_WIDTH), 0)

    def group(g, c):
        off = pl.multiple_of(g * SUBLANES, SUBLANES)
        av = a_s[pl.ds(off, SUBLANES), :]
        bv = b_s[pl.ds(off, SUBLANES), :]
        for d in (1, 2, 4):
            a_sh = jnp.where(row >= d, pltpu.roll(av, d, 0), 1.0)
            b_sh = jnp.where(row >= d, pltpu.roll(bv, d, 0), 0.0)
            bv = av * b_sh + bv
            av = av * a_sh
        hv = bv + av * c
        b_s[pl.ds(off, SUBLANES), :] = hv
        return jnp.broadcast_to(hv[SUBLANES - 1:SUBLANES, :], (SUBLANES, LRU_WIDTH))

    carry[...] = lax.fori_loop(0, tt // SUBLANES, group, carry[...], unroll=4)

    y = b_s[...] * _gelu(gr_ref[...].astype(F32))
    out_ref[...] = (_rms(y, NORM_EPS) * gn_ref[...]).astype(BF16)


def _lru(xr, gr, cw, cb, wa, ba, wx, bx, ap, gn, tt):
    B, TP, _ = xr.shape
    rows = pl.BlockSpec((None, tt, LRU_WIDTH), lambda b, i: (b, i, 0))
    vec = _full((1, LRU_WIDTH))
    wspec = _full((LRU_WIDTH // GATE_TILE, GATE_TILE, GATE_TILE))
    return pl.pallas_call(
        _lru_kernel,
        grid=(B, TP // tt),
        in_specs=[rows, rows, _full((LRU_CONV, LRU_WIDTH)), vec, wspec, vec, wspec, vec, vec, vec],
        out_specs=rows,
        out_shape=jax.ShapeDtypeStruct((B, TP, LRU_WIDTH), BF16),
        scratch_shapes=[pltpu.VMEM((tt + SUBLANES, LRU_WIDTH), F32),
                        pltpu.VMEM((tt, LRU_WIDTH), F32),
                        pltpu.VMEM((tt, LRU_WIDTH), F32),
                        pltpu.VMEM((SUBLANES, LRU_WIDTH), F32),
                        pltpu.VMEM((SUBLANES, LRU_WIDTH), F32)],
        compiler_params=_params("parallel", "arbitrary"),
        name="lru",
    )(xr, gr, cw, cb, wa, ba, wx, bx, ap, gn)


def _visible_end(p):
    return jnp.where(p < N_META, N_META,
                     N_META + CHUNK + ((p - N_META) & ~(CHUNK - 1)))


def _attn_kernel(qt_ref, k_ref, v_ref, lam_ref, sub_ref, out_ref, m_s, l_s, acc_s,
                 *, lambda_init, tq, tail_keys):
    i = pl.program_id(2)
    nq = pl.num_programs(2)
    qt = qt_ref[...]
    dim = lax.broadcasted_iota(jnp.int32, (V_DIM, tq), 0)
    zero = jnp.zeros_like(qt)
    qcat = jnp.concatenate([jnp.where(dim < HEAD_DIM, qt, zero),
                            jnp.where(dim >= HEAD_DIM, qt, zero)], axis=1)

    m_s[...] = jnp.full_like(m_s, NEG_INF)
    l_s[...] = jnp.zeros_like(l_s)
    acc_s[...] = jnp.zeros_like(acc_s)

    qpos = i * tq + lax.broadcasted_iota(jnp.int32, (1, tq), 1)
    qend = _visible_end(qpos)
    qend2 = jnp.concatenate([qend, qend], axis=1)

    def step(start, size, masked):
        kb = k_ref[pl.ds(start, size), :]
        vb = v_ref[pl.ds(start, size), :]
        s = _dot(kb, qcat)
        if masked:
            kpos = start + lax.broadcasted_iota(jnp.int32, (size, 1), 0)
            s = jnp.where(kpos < qend2, s, NEG_INF)
        m_old = m_s[...]
        m_new = jnp.maximum(m_old, jnp.max(s, axis=0, keepdims=True))
        alpha = jnp.exp(m_old - m_new)
        p = jnp.exp(s - m_new)
        l_s[...] = alpha * l_s[...] + jnp.sum(p, axis=0, keepdims=True)
        m_s[...] = m_new
        pv = lax.dot_general(vb, p.astype(BF16), (((0,), (0,)), ((), ())),
                             preferred_element_type=F32)
        acc_s[...] = alpha * acc_s[...] + pv

    def full_block(j, c):
        step(pl.multiple_of(j * tq, tq), tq, False)
        return c

    lax.fori_loop(0, i, full_block, 0)
    step(pl.multiple_of(i * tq, tq), tq, True)

    @pl.when(i < nq - 1)
    def _():
        step(pl.multiple_of((i + 1) * tq, tq), tail_keys, True)

    lp = lam_ref[...]
    lam = (jnp.exp(jnp.sum(lp[0:1] * lp[1:2], axis=1, keepdims=True))
           - jnp.exp(jnp.sum(lp[2:3] * lp[3:4], axis=1, keepdims=True)) + lambda_init)
    o = acc_s[...] / l_s[...]
    o = o[:, :tq] - lam * o[:, tq:]
    o = o * lax.rsqrt(jnp.mean(o * o, axis=0, keepdims=True) + SUBLN_EPS)
    out_ref[...] = (o.T * sub_ref[...] * (1.0 - lambda_init)).astype(BF16)


def _attn(qt, k, v, lam, sub, lambda_init, tq):
    B, TP, _ = k.shape
    kern = functools.partial(_attn_kernel, lambda_init=lambda_init, tq=tq, tail_keys=LANES)
    return pl.pallas_call(
        kern,
        grid=(B, HEADS, TP // tq),
        in_specs=[pl.BlockSpec((None, V_DIM, tq), lambda b, h, i: (b, h, i)),
                  pl.BlockSpec((None, TP, V_DIM), lambda b, h, i: (b, 0, h)),
                  pl.BlockSpec((None, TP, V_DIM), lambda b, h, i: (b, 0, h)),
                  _full((4, HEAD_DIM)), _full((1, V_DIM))],
        out_specs=pl.BlockSpec((None, tq, V_DIM), lambda b, h, i: (b, i, h)),
        out_shape=jax.ShapeDtypeStruct((B, TP, ATT_WIDTH), BF16),
        scratch_shapes=[pltpu.VMEM((1, 2 * tq), F32), pltpu.VMEM((1, 2 * tq), F32),
                        pltpu.VMEM((V_DIM, 2 * tq), F32)],
        compiler_params=_params("parallel", "parallel", "arbitrary"),
        name="diff_attn",
    )(qt, k, v, lam, sub)


def _out_proj_kernel(h_ref, yr_ref, ya_ref, w_ref, g_ref, out_ref):
    y = _dot(yr_ref[...], w_ref[0:LRU_WIDTH, :]) + _dot(ya_ref[...], w_ref[LRU_WIDTH:, :])
    out_ref[...] = h_ref[...] + _rms(y, NORM_EPS) * g_ref[...]


def _out_proj(h, yr, ya, w, g, tm):
    B, TP, _ = h.shape
    rows = lambda width: pl.BlockSpec((None, tm, width), lambda b, i: (b, i, 0))
    return pl.pallas_call(
        _out_proj_kernel,
        grid=(B, TP // tm),
        in_specs=[rows(D_MODEL), rows(LRU_WIDTH), rows(ATT_WIDTH),
                  _full((D_MODEL, D_MODEL)), _full((1, D_MODEL))],
        out_specs=rows(D_MODEL),
        out_shape=jax.ShapeDtypeStruct(h.shape, F32),
        compiler_params=_params("parallel", "parallel"),
        name="out_proj",
    )(h, yr, ya, w, g)


def _ffn_kernel(h_ref, gpre_ref, wg_ref, wv_ref, cw_ref, cb_ref, wd_ref, gpost_ref,
                out_ref, uext, g_s, acc_s, *, tf):
    tm = h_ref.shape[0]
    halo = BF16_ROWS

    @pl.when(pl.program_id(1) == 0)
    def _():
        uext[0:halo, :] = jnp.zeros((halo, D_MODEL), BF16)

    @pl.when(pl.program_id(1) > 0)
    def _():
        uext[0:halo, :] = uext[tm:tm + halo, :]

    h = h_ref[...]
    uext[halo:, :] = (_rms(h, NORM_EPS) * gpre_ref[...]).astype(BF16)

    for c in range(D_FF // tf):
        cols = slice(c * tf, (c + 1) * tf)
        g_s[...] = _dot(uext[...], wg_ref[:, cols])
        val = _dot(uext[halo:, :], wv_ref[:, cols])
        gc = cb_ref[:, cols]
        for k in range(FFN_CONV):
            off = halo - (FFN_CONV - 1) + k
            gc = gc + g_s[off:off + tm, :] * cw_ref[k:k + 1, cols]
        act = (_gelu(gc) * val).astype(BF16)
        part = _dot(act, wd_ref[cols, :])
        if c == 0:
            acc_s[...] = part
        else:
            acc_s[...] += part

    out_ref[...] = h + _rms(acc_s[...], NORM_EPS) * gpost_ref[...]


def _ffn(h, gpre, wg, wv, cw, cb, wd, gpost, tm, tf):
    B, TP, _ = h.shape
    rows = pl.BlockSpec((None, tm, D_MODEL), lambda b, i: (b, i, 0))
    once = pl.Buffered(1)
    const = lambda shape: pl.BlockSpec(shape, lambda b, i: (0, 0), pipeline_mode=once)
    return pl.pallas_call(
        functools.partial(_ffn_kernel, tf=tf),
        grid=(B, TP // tm),
        in_specs=[rows, _full((1, D_MODEL)), const((D_MODEL, D_FF)), const((D_MODEL, D_FF)),
                  _full((FFN_CONV, D_FF)), _full((1, D_FF)), const((D_FF, D_MODEL)),
                  _full((1, D_MODEL))],
        out_specs=rows,
        out_shape=jax.ShapeDtypeStruct(h.shape, F32),
        scratch_shapes=[pltpu.VMEM((tm + BF16_ROWS, D_MODEL), BF16),
                        pltpu.VMEM((tm + BF16_ROWS, tf), F32),
                        pltpu.VMEM((tm, D_MODEL), F32)],
        compiler_params=_params("parallel", "arbitrary"),
        name="ffn",
    )(h, gpre, wg, wv, cw, cb, wd, gpost)


def _block_diag(w):
    per = GATE_TILE // (LRU_WIDTH // LRU_BLOCKS)
    bd = w.shape[-1]
    w = w.reshape(LRU_WIDTH // GATE_TILE, per, bd, bd)
    eye = jnp.eye(per, dtype=w.dtype)
    out = jnp.einsum('jpcd,pq->jpcqd', w, eye)
    return out.reshape(LRU_WIDTH // GATE_TILE, GATE_TILE, GATE_TILE).astype(BF16)


def _rope_tables(tp):
    half = HEAD_DIM // 2
    inv = 1.0 / (ROPE_THETA ** (jnp.arange(0, HEAD_DIM, 2, dtype=F32) / HEAD_DIM))
    ang = jnp.arange(tp, dtype=F32)[:, None] * inv[None, :]
    cos, sin = jnp.cos(ang), jnp.sin(ang)
    reps = LANES // HEAD_DIM
    return (jnp.tile(jnp.concatenate([cos, cos], axis=1), (1, reps)),
            jnp.tile(jnp.concatenate([-sin, sin], axis=1), (1, reps)))


def kernel(x, meta_tokens, ln_mix_pre, ln_mix_post, ln_ffn_pre, ln_ffn_post, w_in, lru_conv_w, lru_conv_b, lru_wa, lru_ba, lru_wx, lru_bx, lru_a_param, lru_out_norm, lam_q1, lam_k1, lam_q2, lam_k2, diff_subln, w_out, w_up, ffn_conv_w, ffn_conv_b, w_down):
    B, S, _ = x.shape
    depth = w_in.shape[0]
    T = N_META + S
    TP = -(-T // LANES) * LANES
    assert TP % 384 == 0 and TP % 528 == 0 and TP % 1056 == 0
    meta = jnp.broadcast_to(meta_tokens[None].astype(x.dtype), (B, N_META, D_MODEL))
    h = jnp.concatenate([meta, x, jnp.zeros((B, TP - T, D_MODEL), x.dtype)], axis=1)
    cos, sin = _rope_tables(TP)
    vec = lambda a: a.reshape(1, -1)

    for l in range(depth):
        lambda_init = 0.8 - 0.6 * math.exp(-0.3 * l)
        xr, gr, qt, k, v = _in_proj(h, vec(ln_mix_pre[l]), w_in[l].astype(BF16), cos, sin, tm=384)
        yr = _lru(xr, gr, lru_conv_w[l], vec(lru_conv_b[l]),
                  _block_diag(lru_wa[l]), vec(lru_ba[l]), _block_diag(lru_wx[l]), vec(lru_bx[l]),
                  vec(lru_a_param[l]), vec(lru_out_norm[l]), tt=528)
        lam = jnp.stack([lam_q1[l], lam_k1[l], lam_q2[l], lam_k2[l]])
        ya = _attn(qt, k, v, lam, vec(diff_subln[l]), lambda_init, tq=384)
        h = _out_proj(h, yr, ya, w_out[l].astype(BF16), vec(ln_mix_post[l]), tm=1056)
        wu = w_up[l].astype(BF16)
        h = _ffn(h, vec(ln_ffn_pre[l]), wu[:, :D_FF], wu[:, D_FF:], ffn_conv_w[l],
                 vec(ffn_conv_b[l]), w_down[l].astype(BF16), vec(ln_ffn_post[l]), tm=528, tf=512)
    return h[:, N_META:T]
```

```python
import functools
import math

import jax
import jax.numpy as jnp
from jax import lax
from jax.experimental import pallas as pl
from jax.experimental.pallas import tpu as pltpu

D_MODEL = 1024
N_META = 16
CHUNK = 64
LRU_WIDTH = 512
LRU_BLOCKS = 8
LRU_CONV = 4
LRU_C = 8.0
ATT_WIDTH = 512
HEADS = 4
HEAD_DIM = 64
V_DIM = 128
ROPE_THETA = 10000.0
D_FF = 3 * D_MODEL
FFN_CONV = 3
NORM_EPS = 1e-6
SUBLN_EPS = 1e-5
IN_WIDTH = 2 * LRU_WIDTH + 3 * ATT_WIDTH
NEG_INF = -1e30

LANES = 128
SUBLANES = 8
BF16_ROWS = 16
GATE_TILE = 256
FRONT = LANES

VMEM_LIMIT = 56 * 1024 * 1024

F32 = jnp.float32
BF16 = jnp.bfloat16


def _dot(a, b):
    return jnp.dot(a, b, preferred_element_type=F32)


def _dot_t(a, b):
    return lax.dot_general(a, b, (((0,), (0,)), ((), ())), preferred_element_type=F32)


def _gelu(x):
    c = math.sqrt(2.0 / math.pi)
    return 0.5 * x * (1.0 + jnp.tanh(c * (x + 0.044715 * (x * x * x))))


def _sigmoid(x):
    return 1.0 / (1.0 + jnp.exp(-x))


def _rms(x, eps):
    return x * lax.rsqrt(jnp.mean(x * x, axis=-1, keepdims=True) + eps)


def _params(*sem):
    return pltpu.CompilerParams(dimension_semantics=sem, vmem_limit_bytes=VMEM_LIMIT)


def _full(shape):
    n = len(shape)
    return pl.BlockSpec(shape, lambda *_: (0,) * n)


def _in_proj_kernel(h_ref, g_ref, w_ref, cos_ref, sin_ref,
                    xr_ref, gr_ref, qt_ref, k_ref, v_ref):
    u = (_rms(h_ref[...], NORM_EPS) * g_ref[...]).astype(BF16)
    xr_ref[...] = _dot(u, w_ref[:, 0:LRU_WIDTH])
    gr_ref[...] = _dot(u, w_ref[:, LRU_WIDTH:2 * LRU_WIDTH]).astype(BF16)
    o = 2 * LRU_WIDTH
    zq = _dot(u, w_ref[:, o:o + ATT_WIDTH])
    zk = _dot(u, w_ref[:, o + ATT_WIDTH:o + 2 * ATT_WIDTH])
    v_ref[...] = _dot(u, w_ref[:, o + 2 * ATT_WIDTH:o + 3 * ATT_WIDTH]).astype(BF16)

    cos = cos_ref[...]
    sin = sin_ref[...]
    tm = cos.shape[0]
    lane = lax.broadcasted_iota(jnp.int32, (tm, LANES), 1)
    low_half = (lane & (HEAD_DIM // 2)) == 0

    def rope(x):
        up = pltpu.roll(x, LANES - HEAD_DIM // 2, 1)
        down = pltpu.roll(x, HEAD_DIM // 2, 1)
        return x * cos + jnp.where(low_half, up, down) * sin

    q_scale = HEAD_DIM ** -0.5 * math.log2(math.e)
    for hd in range(HEADS):
        sl = slice(hd * LANES, (hd + 1) * LANES)
        qh = rope(zq[:, sl]) * q_scale
        qt_ref[sl, :] = qh.T.astype(BF16)
        k_ref[:, sl] = rope(zk[:, sl]).astype(BF16)


def _in_proj(h, g, w, cos, sin, tm):
    B, TP, _ = h.shape
    row = lambda width, dt: jax.ShapeDtypeStruct((B, TP, width), dt)
    rows = lambda width: pl.BlockSpec((None, tm, width), lambda b, i: (b, i, 0))
    return pl.pallas_call(
        _in_proj_kernel,
        grid=(B, TP // tm),
        in_specs=[rows(D_MODEL), _full((1, D_MODEL)), _full((D_MODEL, IN_WIDTH)),
                  pl.BlockSpec((tm, LANES), lambda b, i: (i, 0)),
                  pl.BlockSpec((tm, LANES), lambda b, i: (i, 0))],
        out_specs=[rows(LRU_WIDTH), rows(LRU_WIDTH),
                   pl.BlockSpec((None, ATT_WIDTH, tm), lambda b, i: (b, 0, i)),
                   rows(ATT_WIDTH), rows(ATT_WIDTH)],
        out_shape=[row(LRU_WIDTH, F32), row(LRU_WIDTH, BF16),
                   jax.ShapeDtypeStruct((B, ATT_WIDTH, TP), BF16),
                   row(ATT_WIDTH, BF16), row(ATT_WIDTH, BF16)],
        compiler_params=_params("parallel", "arbitrary"),
        name="in_proj",
    )(h, g, w, cos, sin)


def _lru_kernel(xr_ref, gr_ref, cw_ref, cb_ref, wa_ref, ba_ref, wx_ref, bx_ref,
                ap_ref, gn_ref, out_ref, xext, a_s, b_s, tail, carry):
    tt = xr_ref.shape[0]
    first = pl.program_id(1) == 0

    @pl.when(first)
    def _():
        tail[...] = jnp.zeros_like(tail)
        carry[...] = jnp.zeros_like(carry)

    xext[0:SUBLANES, :] = tail[...]
    xext[SUBLANES:, :] = xr_ref[...]
    tail[...] = xr_ref[tt - SUBLANES:, :]

    @pl.when(first)
    def _():
        xext[FRONT:FRONT + SUBLANES, :] = xr_ref[N_META - SUBLANES:N_META, :]

    xc = cb_ref[...]
    for k in range(LRU_CONV):
        off = SUBLANES - (LRU_CONV - 1) + k
        xc = xc + xext[off:off + tt, :] * cw_ref[k:k + 1, :]

    xcb = xc.astype(BF16)

    def gate(w_ref, b_ref):
        parts = [_dot(xcb[:, j * GATE_TILE:(j + 1) * GATE_TILE], w_ref[j])
                 for j in range(LRU_WIDTH // GATE_TILE)]
        return _sigmoid(jnp.concatenate(parts, axis=1) + b_ref[...])

    r = gate(wa_ref, ba_ref)
    gi = gate(wx_ref, bx_ref)
    ap = ap_ref[...]
    log_sig = jnp.minimum(ap, 0.0) - jnp.log1p(jnp.exp(-jnp.abs(ap)))
    a = jnp.exp(LRU_C * r * log_sig)
    a_s[...] = a
    b_s[...] = jnp.sqrt(1.0 - a * a) * (gi * xc)

    @pl.when(first)
    def _():
        a_s[N_META:FRONT, :] = jnp.ones((FRONT - N_META, LRU_WIDTH), F32)
        b_s[N_META:FRONT, :] = jnp.zeros((FRONT - N_META, LRU_WIDTH), F32)

    row = lax.broadcasted_iota(jnp.int32, (SUBLANES, LRU_WIDTH), 0)

    def group(g, c):
        off = pl.multiple_of(g * SUBLANES, SUBLANES)
        av = a_s[pl.ds(off, SUBLANES), :]
        bv = b_s[pl.ds(off, SUBLANES), :]
        for d in (1, 2, 4):
            a_sh = jnp.where(row >= d, pltpu.roll(av, d, 0), 1.0)
            b_sh = jnp.where(row >= d, pltpu.roll(bv, d, 0), 0.0)
            bv = av * b_sh + bv
            av = av * a_sh
        hv = bv + av * c
        b_s[pl.ds(off, SUBLANES), :] = hv
        return jnp.broadcast_to(hv[SUBLANES - 1:SUBLANES, :], (SUBLANES, LRU_WIDTH))

    carry[...] = lax.fori_loop(0, tt // SUBLANES, group, carry[...], unroll=4)

    y = b_s[...] * _gelu(gr_ref[...].astype(F32))
    out_ref[...] = (_rms(y, NORM_EPS) * gn_ref[...]).astype(BF16)


def _lru(xr, gr, cw, cb, wa, ba, wx, bx, ap, gn, tt):
    B, TP, _ = xr.shape
    rows = pl.BlockSpec((None, tt, LRU_WIDTH), lambda b, i: (b, i, 0))
    vec = _full((1, LRU_WIDTH))
    wspec = _full((LRU_WIDTH // GATE_TILE, GATE_TILE, GATE_TILE))
    return pl.pallas_call(
        _lru_kernel,
        grid=(B, TP // tt),
        in_specs=[rows, rows, _full((LRU_CONV, LRU_WIDTH)), vec, wspec, vec, wspec, vec, vec, vec],
        out_specs=rows,
        out_shape=jax.ShapeDtypeStruct((B, TP, LRU_WIDTH), BF16),
        scratch_shapes=[pltpu.VMEM((tt + SUBLANES, LRU_WIDTH), F32),
                        pltpu.VMEM((tt, LRU_WIDTH), F32),
                        pltpu.VMEM((tt, LRU_WIDTH), F32),
                        pltpu.VMEM((SUBLANES, LRU_WIDTH), F32),
                        pltpu.VMEM((SUBLANES, LRU_WIDTH), F32)],
        compiler_params=_params("parallel", "arbitrary"),
        name="lru",
    )(xr, gr, cw, cb, wa, ba, wx, bx, ap, gn)


def _attn_kernel(qt_ref, k_ref, v_ref, lam_ref, sub_ref, out_ref,
                 qc_s, s0, s1, m_s, l_s, acc_s, *, lambda_init, tq):
    n_tiles = (qt_ref.shape[1] - FRONT) // tq
    lp = lam_ref[...]
    lam = (jnp.exp(jnp.sum(lp[0:1] * lp[1:2], axis=1, keepdims=True))
           - jnp.exp(jnp.sum(lp[2:3] * lp[3:4], axis=1, keepdims=True)) + lambda_init)
    gain = sub_ref[...] * (1.0 - lambda_init)
    k_meta = k_ref[0:N_META, :]
    v_meta = v_ref[0:N_META, :]

    def qcat(qt):
        dim = lax.broadcasted_iota(jnp.int32, qt.shape, 0)
        zero = jnp.zeros_like(qt)
        return jnp.concatenate([jnp.where(dim < HEAD_DIM, qt, zero),
                                jnp.where(dim >= HEAD_DIM, qt, zero)], axis=1)

    def meta_state(qc):
        s = _dot(k_meta, qc)
        m = jnp.max(s, axis=0, keepdims=True)
        p = jnp.exp2(s - m)
        return m, jnp.sum(p, axis=0, keepdims=True), _dot_t(v_meta, p.astype(BF16))

    def finish(acc, l, rows):
        n = acc.shape[1] // 2
        o = acc * (1.0 / l)
        o = o[:, :n] - lam * o[:, n:]
        o = o * lax.rsqrt(jnp.mean(o * o, axis=0, keepdims=True) + SUBLN_EPS)
        out_ref[rows, :] = (o.T * gain).astype(BF16)

    def scores(s_ref, j):
        start = pl.multiple_of(FRONT + j * tq, LANES)
        s_ref[...] = _dot(k_ref[pl.ds(start, tq), :], qc_s[...])

    def absorb(s_ref, j, masked):
        s = s_ref[...]
        if masked:
            kc = lax.broadcasted_iota(jnp.int32, (tq, 1), 0) // CHUNK
            qc_ = (lax.broadcasted_iota(jnp.int32, (1, 2 * tq), 1) % tq) // CHUNK
            s = jnp.where(kc <= qc_, s, NEG_INF)
        m_old = m_s[...]
        m_new = jnp.maximum(m_old, jnp.max(s, axis=0, keepdims=True))
        alpha = jnp.exp2(m_old - m_new)
        p = jnp.exp2(s - m_new)
        l_s[...] = alpha * l_s[...] + jnp.sum(p, axis=0, keepdims=True)
        m_s[...] = m_new
        vb = v_ref[pl.ds(pl.multiple_of(FRONT + j * tq, LANES), tq), :]
        acc_s[...] = alpha * acc_s[...] + _dot_t(vb, p.astype(BF16))

    m, l, acc = meta_state(qcat(qt_ref[:, 0:FRONT]))
    finish(acc, l, slice(0, FRONT))

    def tile(i, c):
        start = pl.multiple_of(FRONT + i * tq, LANES)
        qc_s[...] = qcat(qt_ref[:, pl.ds(start, tq)])
        m, l, acc = meta_state(qc_s[...])
        m_s[...] = m
        l_s[...] = l
        acc_s[...] = acc
        scores(s0, 0)

        def pair(jj, c2):
            j = 2 * jj
            scores(s1, j + 1)
            absorb(s0, j, False)
            scores(s0, j + 2)
            absorb(s1, j + 1, False)
            return c2

        lax.fori_loop(0, i // 2, pair, 0)

        @pl.when(i % 2 == 1)
        def _():
            scores(s1, i)
            absorb(s0, i - 1, False)
            absorb(s1, i, True)

        @pl.when(i % 2 == 0)
        def _():
            absorb(s0, i, True)

        finish(acc_s[...], l_s[...], pl.ds(start, tq))
        return c

    lax.fori_loop(0, n_tiles, tile, 0)


def _attn(qt, k, v, lam, sub, lambda_init, tq):
    B, TP, _ = k.shape
    kern = functools.partial(_attn_kernel, lambda_init=lambda_init, tq=tq)
    return pl.pallas_call(
        kern,
        grid=(B, HEADS),
        in_specs=[pl.BlockSpec((None, V_DIM, TP), lambda b, h: (b, h, 0)),
                  pl.BlockSpec((None, TP, V_DIM), lambda b, h: (b, 0, h)),
                  pl.BlockSpec((None, TP, V_DIM), lambda b, h: (b, 0, h)),
                  _full((4, HEAD_DIM)), _full((1, V_DIM))],
        out_specs=pl.BlockSpec((None, TP, V_DIM), lambda b, h: (b, 0, h)),
        out_shape=jax.ShapeDtypeStruct((B, TP, ATT_WIDTH), BF16),
        scratch_shapes=[pltpu.VMEM((V_DIM, 2 * tq), BF16),
                        pltpu.VMEM((tq, 2 * tq), F32), pltpu.VMEM((tq, 2 * tq), F32),
                        pltpu.VMEM((1, 2 * tq), F32), pltpu.VMEM((1, 2 * tq), F32),
                        pltpu.VMEM((V_DIM, 2 * tq), F32)],
        compiler_params=_params("parallel", "parallel"),
        name="diff_attn",
    )(qt, k, v, lam, sub)


def _out_proj_kernel(h_ref, yr_ref, ya_ref, w_ref, g_ref, out_ref):
    y = _dot(yr_ref[...], w_ref[0:LRU_WIDTH, :]) + _dot(ya_ref[...], w_ref[LRU_WIDTH:, :])
    out_ref[...] = h_ref[...] + _rms(y, NORM_EPS) * g_ref[...]


def _out_proj(h, yr, ya, w, g, tm):
    B, TP, _ = h.shape
    rows = lambda width: pl.BlockSpec((None, tm, width), lambda b, i: (b, i, 0))
    return pl.pallas_call(
        _out_proj_kernel,
        grid=(B, TP // tm),
        in_specs=[rows(D_MODEL), rows(LRU_WIDTH), rows(ATT_WIDTH),
                  _full((D_MODEL, D_MODEL)), _full((1, D_MODEL))],
        out_specs=rows(D_MODEL),
        out_shape=jax.ShapeDtypeStruct(h.shape, F32),
        compiler_params=_params("parallel", "parallel"),
        name="out_proj",
    )(h, yr, ya, w, g)


def _ffn_kernel(h_ref, gpre_ref, wg_ref, wv_ref, cw_ref, cb_ref, wd_ref, gpost_ref,
                out_ref, uext, g_s, acc_s, *, tf):
    tm = h_ref.shape[0]
    halo = BF16_ROWS
    first = pl.program_id(1) == 0

    @pl.when(first)
    def _():
        uext[0:halo, :] = jnp.zeros((halo, D_MODEL), BF16)

    @pl.when(jnp.logical_not(first))
    def _():
        uext[0:halo, :] = uext[tm:tm + halo, :]

    h = h_ref[...]
    uext[halo:, :] = (_rms(h, NORM_EPS) * gpre_ref[...]).astype(BF16)

    @pl.when(first)
    def _():
        uext[halo + FRONT - N_META:halo + FRONT, :] = uext[halo:halo + N_META, :]

    for c in range(D_FF // tf):
        cols = slice(c * tf, (c + 1) * tf)
        g_s[...] = _dot(uext[...], wg_ref[:, cols])
        val = _dot(uext[halo:, :], wv_ref[:, cols])
        gc = cb_ref[:, cols]
        for k in range(FFN_CONV):
            off = halo - (FFN_CONV - 1) + k
            gc = gc + g_s[off:off + tm, :] * cw_ref[k:k + 1, cols]
        act = (_gelu(gc) * val).astype(BF16)
        part = _dot(act, wd_ref[cols, :])
        if c == 0:
            acc_s[...] = part
        else:
            acc_s[...] += part

    out_ref[...] = h + _rms(acc_s[...], NORM_EPS) * gpost_ref[...]


def _ffn(h, gpre, wg, wv, cw, cb, wd, gpost, tm, tf):
    B, TP, _ = h.shape
    rows = pl.BlockSpec((None, tm, D_MODEL), lambda b, i: (b, i, 0))
    once = pl.Buffered(1)
    const = lambda shape: pl.BlockSpec(shape, lambda b, i: (0, 0), pipeline_mode=once)
    return pl.pallas_call(
        functools.partial(_ffn_kernel, tf=tf),
        grid=(B, TP // tm),
        in_specs=[rows, _full((1, D_MODEL)), const((D_MODEL, D_FF)), const((D_MODEL, D_FF)),
                  _full((FFN_CONV, D_FF)), _full((1, D_FF)), const((D_FF, D_MODEL)),
                  _full((1, D_MODEL))],
        out_specs=rows,
        out_shape=jax.ShapeDtypeStruct(h.shape, F32),
        scratch_shapes=[pltpu.VMEM((tm + BF16_ROWS, D_MODEL), BF16),
                        pltpu.VMEM((tm + BF16_ROWS, tf), F32),
                        pltpu.VMEM((tm, D_MODEL), F32)],
        compiler_params=_params("parallel", "arbitrary"),
        name="ffn",
    )(h, gpre, wg, wv, cw, cb, wd, gpost)


def _block_diag(w):
    per = GATE_TILE // (LRU_WIDTH // LRU_BLOCKS)
    bd = w.shape[-1]
    w = w.reshape(LRU_WIDTH // GATE_TILE, per, bd, bd)
    eye = jnp.eye(per, dtype=w.dtype)
    out = jnp.einsum('jpcd,pq->jpcqd', w, eye)
    return out.reshape(LRU_WIDTH // GATE_TILE, GATE_TILE, GATE_TILE).astype(BF16)


def _rope_tables(seq):
    inv = 1.0 / (ROPE_THETA ** (jnp.arange(0, HEAD_DIM, 2, dtype=F32) / HEAD_DIM))
    pos = jnp.concatenate([jnp.arange(N_META), jnp.zeros(FRONT - N_META, jnp.int32),
                           N_META + jnp.arange(seq)]).astype(F32)
    ang = pos[:, None] * inv[None, :]
    cos, sin = jnp.cos(ang), jnp.sin(ang)
    reps = LANES // HEAD_DIM
    return (jnp.tile(jnp.concatenate([cos, cos], axis=1), (1, reps)),
            jnp.tile(jnp.concatenate([-sin, sin], axis=1), (1, reps)))


def kernel(x, meta_tokens, ln_mix_pre, ln_mix_post, ln_ffn_pre, ln_ffn_post, w_in, lru_conv_w, lru_conv_b, lru_wa, lru_ba, lru_wx, lru_bx, lru_a_param, lru_out_norm, lam_q1, lam_k1, lam_q2, lam_k2, diff_subln, w_out, w_up, ffn_conv_w, ffn_conv_b, w_down):
    B, S, _ = x.shape
    depth = w_in.shape[0]
    TP = FRONT + S
    assert TP % 384 == 0 and TP % 528 == 0 and TP % 1056 == 0 and S % 512 == 0
    meta = jnp.broadcast_to(meta_tokens[None].astype(x.dtype), (B, N_META, D_MODEL))
    h = jnp.concatenate([meta, jnp.zeros((B, FRONT - N_META, D_MODEL), x.dtype), x], axis=1)
    cos, sin = _rope_tables(S)
    vec = lambda a: a.reshape(1, -1)

    for l in range(depth):
        lambda_init = 0.8 - 0.6 * math.exp(-0.3 * l)
        xr, gr, qt, k, v = _in_proj(h, vec(ln_mix_pre[l]), w_in[l].astype(BF16), cos, sin, tm=384)
        yr = _lru(xr, gr, lru_conv_w[l], vec(lru_conv_b[l]),
                  _block_diag(lru_wa[l]), vec(lru_ba[l]), _block_diag(lru_wx[l]), vec(lru_bx[l]),
                  vec(lru_a_param[l]), vec(lru_out_norm[l]), tt=528)
        lam = jnp.stack([lam_q1[l], lam_k1[l], lam_q2[l], lam_k2[l]])
        ya = _attn(qt, k, v, lam, vec(diff_subln[l]), lambda_init, tq=512)
        h = _out_proj(h, yr, ya, w_out[l].astype(BF16), vec(ln_mix_post[l]), tm=1056)
        wu = w_up[l].astype(BF16)
        h = _ffn(h, vec(ln_ffn_pre[l]), wu[:, :D_FF], wu[:, D_FF:], ffn_conv_w[l],
                 vec(ffn_conv_b[l]), w_down[l].astype(BF16), vec(ln_ffn_post[l]), tm=528, tf=512)
    return h[:, FRONT:]
```

```python
import functools
import math

import jax
import jax.numpy as jnp
from jax import lax
from jax.experimental import pallas as pl
from jax.experimental.pallas import tpu as pltpu

D_MODEL = 1024
N_META = 16
CHUNK = 64
LRU_WIDTH = 512
LRU_BLOCKS = 8
LRU_CONV = 4
LRU_C = 8.0
ATT_WIDTH = 512
HEADS = 4
HEAD_DIM = 64
V_DIM = 128
ROPE_THETA = 10000.0
D_FF = 3 * D_MODEL
FFN_CONV = 3
NORM_EPS = 1e-6
SUBLN_EPS = 1e-5
IN_WIDTH = 2 * LRU_WIDTH + 3 * ATT_WIDTH
NEG_INF = -1e30

LANES = 128
SUBLANES = 8
BF16_ROWS = 16
GATE_TILE = 256
FRONT = LANES

VMEM_LIMIT = 56 * 1024 * 1024

F32 = jnp.float32
BF16 = jnp.bfloat16


def _dot(a, b):
    return jnp.dot(a, b, preferred_element_type=F32)


def _dot_t(a, b):
    return lax.dot_general(a, b, (((0,), (0,)), ((), ())), preferred_element_type=F32)


def _gelu(x):
    c = math.sqrt(2.0 / math.pi)
    return 0.5 * x * (1.0 + jnp.tanh(c * (x + 0.044715 * (x * x * x))))


def _sigmoid(x):
    return 1.0 / (1.0 + jnp.exp(-x))


def _rms(x, eps):
    return x * lax.rsqrt(jnp.mean(x * x, axis=-1, keepdims=True) + eps)


def _params(*sem):
    return pltpu.CompilerParams(dimension_semantics=sem, vmem_limit_bytes=VMEM_LIMIT)


def _full(shape):
    n = len(shape)
    return pl.BlockSpec(shape, lambda *_: (0,) * n)


def _in_proj_kernel(h_ref, g_ref, w_ref, cos_ref, sin_ref,
                    xr_ref, gr_ref, qt_ref, k_ref, v_ref):
    u = (_rms(h_ref[...], NORM_EPS) * g_ref[...]).astype(BF16)
    xr_ref[...] = _dot(u, w_ref[:, 0:LRU_WIDTH])
    gr_ref[...] = _dot(u, w_ref[:, LRU_WIDTH:2 * LRU_WIDTH]).astype(BF16)
    o = 2 * LRU_WIDTH
    zq = _dot(u, w_ref[:, o:o + ATT_WIDTH])
    zk = _dot(u, w_ref[:, o + ATT_WIDTH:o + 2 * ATT_WIDTH])
    v_ref[...] = _dot(u, w_ref[:, o + 2 * ATT_WIDTH:o + 3 * ATT_WIDTH]).astype(BF16)

    cos = cos_ref[...]
    sin = sin_ref[...]
    tm = cos.shape[0]
    lane = lax.broadcasted_iota(jnp.int32, (tm, LANES), 1)
    low_half = (lane & (HEAD_DIM // 2)) == 0

    def rope(x):
        up = pltpu.roll(x, LANES - HEAD_DIM // 2, 1)
        down = pltpu.roll(x, HEAD_DIM // 2, 1)
        return x * cos + jnp.where(low_half, up, down) * sin

    q_scale = HEAD_DIM ** -0.5 * math.log2(math.e)
    for hd in range(HEADS):
        sl = slice(hd * LANES, (hd + 1) * LANES)
        qh = rope(zq[:, sl]) * q_scale
        qt_ref[sl, :] = qh.T.astype(BF16)
        k_ref[:, sl] = rope(zk[:, sl]).astype(BF16)


def _in_proj(h, g, w, cos, sin, tm):
    B, TP, _ = h.shape
    row = lambda width, dt: jax.ShapeDtypeStruct((B, TP, width), dt)
    rows = lambda width: pl.BlockSpec((None, tm, width), lambda b, i: (b, i, 0))
    return pl.pallas_call(
        _in_proj_kernel,
        grid=(B, TP // tm),
        in_specs=[rows(D_MODEL), _full((1, D_MODEL)), _full((D_MODEL, IN_WIDTH)),
                  pl.BlockSpec((tm, LANES), lambda b, i: (i, 0)),
                  pl.BlockSpec((tm, LANES), lambda b, i: (i, 0))],
        out_specs=[rows(LRU_WIDTH), rows(LRU_WIDTH),
                   pl.BlockSpec((None, ATT_WIDTH, tm), lambda b, i: (b, 0, i)),
                   rows(ATT_WIDTH), rows(ATT_WIDTH)],
        out_shape=[row(LRU_WIDTH, F32), row(LRU_WIDTH, BF16),
                   jax.ShapeDtypeStruct((B, ATT_WIDTH, TP), BF16),
                   row(ATT_WIDTH, BF16), row(ATT_WIDTH, BF16)],
        compiler_params=_params("parallel", "arbitrary"),
        name="in_proj",
    )(h, g, w, cos, sin)


def _lru_kernel(xr_ref, gr_ref, cw_ref, cb_ref, wa_ref, ba_ref, wx_ref, bx_ref,
                ap_ref, gn_ref, out_ref, xext, a_s, b_s, tail, carry):
    tt = xr_ref.shape[0]
    first = pl.program_id(1) == 0

    @pl.when(first)
    def _():
        tail[...] = jnp.zeros_like(tail)
        carry[...] = jnp.zeros_like(carry)

    xext[0:SUBLANES, :] = tail[...]
    xext[SUBLANES:, :] = xr_ref[...]
    tail[...] = xr_ref[tt - SUBLANES:, :]

    @pl.when(first)
    def _():
        xext[FRONT:FRONT + SUBLANES, :] = xr_ref[N_META - SUBLANES:N_META, :]

    xc = cb_ref[...]
    for k in range(LRU_CONV):
        off = SUBLANES - (LRU_CONV - 1) + k
        xc = xc + xext[off:off + tt, :] * cw_ref[k:k + 1, :]

    xcb = xc.astype(BF16)

    def gate(w_ref, b_ref):
        parts = [_dot(xcb[:, j * GATE_TILE:(j + 1) * GATE_TILE], w_ref[j])
                 for j in range(LRU_WIDTH // GATE_TILE)]
        return _sigmoid(jnp.concatenate(parts, axis=1) + b_ref[...])

    r = gate(wa_ref, ba_ref)
    gi = gate(wx_ref, bx_ref)
    ap = ap_ref[...]
    log_sig = jnp.minimum(ap, 0.0) - jnp.log1p(jnp.exp(-jnp.abs(ap)))
    a = jnp.exp(LRU_C * r * log_sig)
    a_s[...] = a
    d = 1.0 - a * a
    b_s[...] = jnp.where(d > 0.0, d * lax.rsqrt(d), 0.0) * (gi * xc)

    @pl.when(first)
    def _():
        a_s[N_META:FRONT, :] = jnp.ones((FRONT - N_META, LRU_WIDTH), F32)
        b_s[N_META:FRONT, :] = jnp.zeros((FRONT - N_META, LRU_WIDTH), F32)

    row = lax.broadcasted_iota(jnp.int32, (SUBLANES, LRU_WIDTH), 0)

    def group(g, c):
        off = pl.multiple_of(g * SUBLANES, SUBLANES)
        av = a_s[pl.ds(off, SUBLANES), :]
        bv = b_s[pl.ds(off, SUBLANES), :]
        for d in (1, 2, 4):
            a_sh = jnp.where(row >= d, pltpu.roll(av, d, 0), 1.0)
            b_sh = jnp.where(row >= d, pltpu.roll(bv, d, 0), 0.0)
            bv = av * b_sh + bv
            av = av * a_sh
        hv = bv + av * c
        b_s[pl.ds(off, SUBLANES), :] = hv
        return jnp.broadcast_to(hv[SUBLANES - 1:SUBLANES, :], (SUBLANES, LRU_WIDTH))

    carry[...] = lax.fori_loop(0, tt // SUBLANES, group, carry[...], unroll=4)

    y = b_s[...] * _gelu(gr_ref[...].astype(F32))
    out_ref[...] = (_rms(y, NORM_EPS) * gn_ref[...]).astype(BF16)


def _lru(xr, gr, cw, cb, wa, ba, wx, bx, ap, gn, tt):
    B, TP, _ = xr.shape
    rows = pl.BlockSpec((None, tt, LRU_WIDTH), lambda b, i: (b, i, 0))
    vec = _full((1, LRU_WIDTH))
    wspec = _full((LRU_WIDTH // GATE_TILE, GATE_TILE, GATE_TILE))
    return pl.pallas_call(
        _lru_kernel,
        grid=(B, TP // tt),
        in_specs=[rows, rows, _full((LRU_CONV, LRU_WIDTH)), vec, wspec, vec, wspec, vec, vec, vec],
        out_specs=rows,
        out_shape=jax.ShapeDtypeStruct((B, TP, LRU_WIDTH), BF16),
        scratch_shapes=[pltpu.VMEM((tt + SUBLANES, LRU_WIDTH), F32),
                        pltpu.VMEM((tt, LRU_WIDTH), F32),
                        pltpu.VMEM((tt, LRU_WIDTH), F32),
                        pltpu.VMEM((SUBLANES, LRU_WIDTH), F32),
                        pltpu.VMEM((SUBLANES, LRU_WIDTH), F32)],
        compiler_params=_params("parallel", "arbitrary"),
        name="lru",
    )(xr, gr, cw, cb, wa, ba, wx, bx, ap, gn)


def _attn_kernel(qt_ref, k_ref, v_ref, lam_ref, sub_ref, out_ref,
                 qc_s, s0, s1, m_s, l_s, acc_s, *, lambda_init, tq):
    n_tiles = (qt_ref.shape[1] - FRONT) // tq
    lp = lam_ref[...]
    lam = (jnp.exp(jnp.sum(lp[0:1] * lp[1:2], axis=1, keepdims=True))
           - jnp.exp(jnp.sum(lp[2:3] * lp[3:4], axis=1, keepdims=True)) + lambda_init)
    gain = sub_ref[...] * (1.0 - lambda_init)
    k_meta = k_ref[0:N_META, :]
    v_meta = v_ref[0:N_META, :]

    def qcat(qt):
        dim = lax.broadcasted_iota(jnp.int32, qt.shape, 0)
        zero = jnp.zeros_like(qt)
        return jnp.concatenate([jnp.where(dim < HEAD_DIM, qt, zero),
                                jnp.where(dim >= HEAD_DIM, qt, zero)], axis=1)

    def meta_state(qc):
        s = _dot(k_meta, qc)
        m = jnp.max(s, axis=0, keepdims=True)
        p = jnp.exp2(s - m)
        return m, jnp.sum(p, axis=0, keepdims=True), _dot_t(v_meta, p.astype(BF16))

    def finish(acc, l, rows):
        n = acc.shape[1] // 2
        o = acc * (1.0 / l)
        o = o[:, :n] - lam * o[:, n:]
        o = o * lax.rsqrt(jnp.mean(o * o, axis=0, keepdims=True) + SUBLN_EPS)
        out_ref[rows, :] = (o.T * gain).astype(BF16)

    def scores(s_ref, j):
        start = pl.multiple_of(FRONT + j * tq, LANES)
        s_ref[...] = _dot(k_ref[pl.ds(start, tq), :], qc_s[...])

    def absorb(s_ref, j, masked):
        s = s_ref[...]
        if masked:
            kc = lax.broadcasted_iota(jnp.int32, (tq, 1), 0) // CHUNK
            qc_ = (lax.broadcasted_iota(jnp.int32, (1, 2 * tq), 1) % tq) // CHUNK
            s = jnp.where(kc <= qc_, s, NEG_INF)
        m_old = m_s[...]
        m_new = jnp.maximum(m_old, jnp.max(s, axis=0, keepdims=True))
        alpha = jnp.exp2(m_old - m_new)
        p = jnp.exp2(s - m_new)
        l_s[...] = alpha * l_s[...] + jnp.sum(p, axis=0, keepdims=True)
        m_s[...] = m_new
        vb = v_ref[pl.ds(pl.multiple_of(FRONT + j * tq, LANES), tq), :]
        acc_s[...] = alpha * acc_s[...] + _dot_t(vb, p.astype(BF16))

    m, l, acc = meta_state(qcat(qt_ref[:, 0:FRONT]))
    finish(acc, l, slice(0, FRONT))

    def tile(i, c):
        start = pl.multiple_of(FRONT + i * tq, LANES)
        qc_s[...] = qcat(qt_ref[:, pl.ds(start, tq)])
        m, l, acc = meta_state(qc_s[...])
        m_s[...] = m
        l_s[...] = l
        acc_s[...] = acc
        scores(s0, 0)

        def pair(jj, c2):
            j = 2 * jj
            scores(s1, j + 1)
            absorb(s0, j, False)
            scores(s0, j + 2)
            absorb(s1, j + 1, False)
            return c2

        lax.fori_loop(0, i // 2, pair, 0)

        @pl.when(i % 2 == 1)
        def _():
            scores(s1, i)
            absorb(s0, i - 1, False)
            absorb(s1, i, True)

        @pl.when(i % 2 == 0)
        def _():
            absorb(s0, i, True)

        finish(acc_s[...], l_s[...], pl.ds(start, tq))
        return c

    lax.fori_loop(0, n_tiles, tile, 0)


def _attn(qt, k, v, lam, sub, lambda_init, tq):
    B, TP, _ = k.shape
    kern = functools.partial(_attn_kernel, lambda_init=lambda_init, tq=tq)
    return pl.pallas_call(
        kern,
        grid=(B, HEADS),
        in_specs=[pl.BlockSpec((None, V_DIM, TP), lambda b, h: (b, h, 0)),
                  pl.BlockSpec((None, TP, V_DIM), lambda b, h: (b, 0, h)),
                  pl.BlockSpec((None, TP, V_DIM), lambda b, h: (b, 0, h)),
                  _full((4, HEAD_DIM)), _full((1, V_DIM))],
        out_specs=pl.BlockSpec((None, TP, V_DIM), lambda b, h: (b, 0, h)),
        out_shape=jax.ShapeDtypeStruct((B, TP, ATT_WIDTH), BF16),
        scratch_shapes=[pltpu.VMEM((V_DIM, 2 * tq), BF16),
                        pltpu.VMEM((tq, 2 * tq), F32), pltpu.VMEM((tq, 2 * tq), F32),
                        pltpu.VMEM((1, 2 * tq), F32), pltpu.VMEM((1, 2 * tq), F32),
                        pltpu.VMEM((V_DIM, 2 * tq), F32)],
        compiler_params=_params("parallel", "parallel"),
        name="diff_attn",
    )(qt, k, v, lam, sub)


def _out_proj_kernel(h_ref, yr_ref, ya_ref, w_ref, g_ref, out_ref):
    y = _dot(yr_ref[...], w_ref[0:LRU_WIDTH, :]) + _dot(ya_ref[...], w_ref[LRU_WIDTH:, :])
    out_ref[...] = h_ref[...] + _rms(y, NORM_EPS) * g_ref[...]


def _out_proj(h, yr, ya, w, g, tm):
    B, TP, _ = h.shape
    rows = lambda width: pl.BlockSpec((None, tm, width), lambda b, i: (b, i, 0))
    return pl.pallas_call(
        _out_proj_kernel,
        grid=(B, TP // tm),
        in_specs=[rows(D_MODEL), rows(LRU_WIDTH), rows(ATT_WIDTH),
                  _full((D_MODEL, D_MODEL)), _full((1, D_MODEL))],
        out_specs=rows(D_MODEL),
        out_shape=jax.ShapeDtypeStruct(h.shape, F32),
        compiler_params=_params("parallel", "parallel"),
        name="out_proj",
    )(h, yr, ya, w, g)


def _ffn_kernel(h_ref, gpre_ref, wg_ref, wv_ref, cw_ref, cb_ref, wd_ref, gpost_ref,
                out_ref, uext, g_a, g_b, v_a, v_b, acc_s, *, tf):
    tm = h_ref.shape[0]
    halo = BF16_ROWS
    first = pl.program_id(1) == 0

    @pl.when(first)
    def _():
        uext[0:halo, :] = jnp.zeros((halo, D_MODEL), BF16)

    @pl.when(jnp.logical_not(first))
    def _():
        uext[0:halo, :] = uext[tm:tm + halo, :]

    h = h_ref[...]
    uext[halo:, :] = (_rms(h, NORM_EPS) * gpre_ref[...]).astype(BF16)

    @pl.when(first)
    def _():
        uext[halo + FRONT - N_META:halo + FRONT, :] = uext[halo:halo + N_META, :]

    n_chunks = D_FF // tf
    bufs = ((g_a, v_a), (g_b, v_b))

    def up(c):
        g_s, v_s = bufs[c % 2]
        cols = slice(c * tf, (c + 1) * tf)
        g_s[...] = _dot(uext[...], wg_ref[:, cols])
        v_s[...] = _dot(uext[halo:, :], wv_ref[:, cols])

    up(0)
    for c in range(n_chunks):
        cols = slice(c * tf, (c + 1) * tf)
        g_s, v_s = bufs[c % 2]
        if c + 1 < n_chunks:
            up(c + 1)
        gc = cb_ref[:, cols]
        for k in range(FFN_CONV):
            off = halo - (FFN_CONV - 1) + k
            gc = gc + g_s[off:off + tm, :] * cw_ref[k:k + 1, cols]
        act = (_gelu(gc) * v_s[...]).astype(BF16)
        part = _dot(act, wd_ref[cols, :])
        if c == 0:
            acc_s[...] = part
        else:
            acc_s[...] += part

    out_ref[...] = h + _rms(acc_s[...], NORM_EPS) * gpost_ref[...]


def _ffn(h, gpre, wg, wv, cw, cb, wd, gpost, tm, tf):
    B, TP, _ = h.shape
    rows = pl.BlockSpec((None, tm, D_MODEL), lambda b, i: (b, i, 0))
    once = pl.Buffered(1)
    const = lambda shape: pl.BlockSpec(shape, lambda b, i: (0, 0), pipeline_mode=once)
    return pl.pallas_call(
        functools.partial(_ffn_kernel, tf=tf),
        grid=(B, TP // tm),
        in_specs=[rows, _full((1, D_MODEL)), const((D_MODEL, D_FF)), const((D_MODEL, D_FF)),
                  _full((FFN_CONV, D_FF)), _full((1, D_FF)), const((D_FF, D_MODEL)),
                  _full((1, D_MODEL))],
        out_specs=rows,
        out_shape=jax.ShapeDtypeStruct(h.shape, F32),
        scratch_shapes=[pltpu.VMEM((tm + BF16_ROWS, D_MODEL), BF16),
                        pltpu.VMEM((tm + BF16_ROWS, tf), F32),
                        pltpu.VMEM((tm + BF16_ROWS, tf), F32),
                        pltpu.VMEM((tm, tf), F32), pltpu.VMEM((tm, tf), F32),
                        pltpu.VMEM((tm, D_MODEL), F32)],
        compiler_params=_params("parallel", "arbitrary"),
        name="ffn",
    )(h, gpre, wg, wv, cw, cb, wd, gpost)


def _block_diag(w):
    per = GATE_TILE // (LRU_WIDTH // LRU_BLOCKS)
    bd = w.shape[-1]
    w = w.reshape(LRU_WIDTH // GATE_TILE, per, bd, bd)
    eye = jnp.eye(per, dtype=w.dtype)
    out = jnp.einsum('jpcd,pq->jpcqd', w, eye)
    return out.reshape(LRU_WIDTH // GATE_TILE, GATE_TILE, GATE_TILE).astype(BF16)


def _rope_tables(seq):
    inv = 1.0 / (ROPE_THETA ** (jnp.arange(0, HEAD_DIM, 2, dtype=F32) / HEAD_DIM))
    pos = jnp.concatenate([jnp.arange(N_META), jnp.zeros(FRONT - N_META, jnp.int32),
                           N_META + jnp.arange(seq)]).astype(F32)
    ang = pos[:, None] * inv[None, :]
    cos, sin = jnp.cos(ang), jnp.sin(ang)
    reps = LANES // HEAD_DIM
    return (jnp.tile(jnp.concatenate([cos, cos], axis=1), (1, reps)),
            jnp.tile(jnp.concatenate([-sin, sin], axis=1), (1, reps)))


def kernel(x, meta_tokens, ln_mix_pre, ln_mix_post, ln_ffn_pre, ln_ffn_post, w_in, lru_conv_w, lru_conv_b, lru_wa, lru_ba, lru_wx, lru_bx, lru_a_param, lru_out_norm, lam_q1, lam_k1, lam_q2, lam_k2, diff_subln, w_out, w_up, ffn_conv_w, ffn_conv_b, w_down):
    B, S, _ = x.shape
    depth = w_in.shape[0]
    TP = FRONT + S
    assert TP % 384 == 0 and TP % 528 == 0 and TP % 1056 == 0 and S % 512 == 0
    meta = jnp.broadcast_to(meta_tokens[None].astype(x.dtype), (B, N_META, D_MODEL))
    h = jnp.concatenate([meta, jnp.zeros((B, FRONT - N_META, D_MODEL), x.dtype), x], axis=1)
    cos, sin = _rope_tables(S)
    vec = lambda a: a.reshape(1, -1)

    for l in range(depth):
        lambda_init = 0.8 - 0.6 * math.exp(-0.3 * l)
        xr, gr, qt, k, v = _in_proj(h, vec(ln_mix_pre[l]), w_in[l].astype(BF16), cos, sin, tm=384)
        yr = _lru(xr, gr, lru_conv_w[l], vec(lru_conv_b[l]),
                  _block_diag(lru_wa[l]), vec(lru_ba[l]), _block_diag(lru_wx[l]), vec(lru_bx[l]),
                  vec(lru_a_param[l]), vec(lru_out_norm[l]), tt=528)
        lam = jnp.stack([lam_q1[l], lam_k1[l], lam_q2[l], lam_k2[l]])
        ya = _attn(qt, k, v, lam, vec(diff_subln[l]), lambda_init, tq=512)
        h = _out_proj(h, yr, ya, w_out[l].astype(BF16), vec(ln_mix_post[l]), tm=1056)
        wu = w_up[l].astype(BF16)
        h = _ffn(h, vec(ln_ffn_pre[l]), wu[:, :D_FF], wu[:, D_FF:], ffn_conv_w[l],
                 vec(ffn_conv_b[l]), w_down[l].astype(BF16), vec(ln_ffn_post[l]), tm=528, tf=512)
    return h[:, FRONT:]
```

```python
import functools
import math

import jax
import jax.numpy as jnp
from jax import lax
from jax.experimental import pallas as pl
from jax.experimental.pallas import tpu as pltpu

D_MODEL = 1024
N_META = 16
CHUNK = 64
LRU_WIDTH = 512
LRU_BLOCKS = 8
LRU_CONV = 4
LRU_C = 8.0
ATT_WIDTH = 512
HEADS = 4
HEAD_DIM = 64
V_DIM = 128
ROPE_THETA = 10000.0
D_FF = 3 * D_MODEL
FFN_CONV = 3
NORM_EPS = 1e-6
SUBLN_EPS = 1e-5
IN_WIDTH = 2 * LRU_WIDTH + 3 * ATT_WIDTH
NEG_INF = -1e30

LANES = 128
SUBLANES = 8
BF16_ROWS = 16
GATE_TILE = 256
FRONT = LANES
PAD = FRONT - N_META

VMEM_LIMIT = 56 * 1024 * 1024

F32 = jnp.float32
BF16 = jnp.bfloat16


def _dot(a, b):
    return jnp.dot(a, b, preferred_element_type=F32)


def _dot_t(a, b):
    return lax.dot_general(a, b, (((0,), (0,)), ((), ())), preferred_element_type=F32)


def _gelu(x):
    c = math.sqrt(2.0 / math.pi)
    return 0.5 * x * (1.0 + jnp.tanh(c * (x + 0.044715 * (x * x * x))))


def _sigmoid(x):
    return 1.0 / (1.0 + jnp.exp(-x))


def _rms(x, eps):
    return x * lax.rsqrt(jnp.mean(x * x, axis=-1, keepdims=True) + eps)


def _params(*sem):
    return pltpu.CompilerParams(dimension_semantics=sem, vmem_limit_bytes=VMEM_LIMIT)


def _full(shape):
    n = len(shape)
    return pl.BlockSpec(shape, lambda *_: (0,) * n)


def _in_proj_kernel(h_ref, g_ref, w_ref, cos_ref, sin_ref,
                    xr_ref, gr_ref, qt_ref, k_ref, v_ref):
    u = (_rms(h_ref[...], NORM_EPS) * g_ref[...]).astype(BF16)
    zx = _dot(u, w_ref[:, 0:LRU_WIDTH])
    for c in range(LRU_WIDTH // LANES):
        xr_ref[c] = zx[:, c * LANES:(c + 1) * LANES]
    gr_ref[...] = _dot(u, w_ref[:, LRU_WIDTH:2 * LRU_WIDTH]).astype(BF16)
    o = 2 * LRU_WIDTH
    zq = _dot(u, w_ref[:, o:o + ATT_WIDTH])
    zk = _dot(u, w_ref[:, o + ATT_WIDTH:o + 2 * ATT_WIDTH])
    v_ref[...] = _dot(u, w_ref[:, o + 2 * ATT_WIDTH:o + 3 * ATT_WIDTH]).astype(BF16)

    cos = cos_ref[...]
    sin = sin_ref[...]
    tm = cos.shape[0]
    lane = lax.broadcasted_iota(jnp.int32, (tm, LANES), 1)
    low_half = (lane & (HEAD_DIM // 2)) == 0

    def rope(x):
        up = pltpu.roll(x, LANES - HEAD_DIM // 2, 1)
        down = pltpu.roll(x, HEAD_DIM // 2, 1)
        return x * cos + jnp.where(low_half, up, down) * sin

    q_scale = HEAD_DIM ** -0.5 * math.log2(math.e)
    for hd in range(HEADS):
        sl = slice(hd * LANES, (hd + 1) * LANES)
        qh = rope(zq[:, sl]) * q_scale
        qt_ref[sl, :] = qh.T.astype(BF16)
        k_ref[:, sl] = rope(zk[:, sl]).astype(BF16)


def _in_proj(h, g, w, cos, sin, tm):
    B, TP, _ = h.shape
    row = lambda width, dt: jax.ShapeDtypeStruct((B, TP, width), dt)
    rows = lambda width: pl.BlockSpec((None, tm, width), lambda b, i: (b, i, 0))
    return pl.pallas_call(
        _in_proj_kernel,
        grid=(B, TP // tm),
        in_specs=[rows(D_MODEL), _full((1, D_MODEL)), _full((D_MODEL, IN_WIDTH)),
                  pl.BlockSpec((tm, LANES), lambda b, i: (i, 0)),
                  pl.BlockSpec((tm, LANES), lambda b, i: (i, 0))],
        out_specs=[pl.BlockSpec((None, LRU_WIDTH // LANES, tm, LANES), lambda b, i: (b, 0, i, 0)),
                   rows(LRU_WIDTH),
                   pl.BlockSpec((None, ATT_WIDTH, tm), lambda b, i: (b, 0, i)),
                   rows(ATT_WIDTH), rows(ATT_WIDTH)],
        out_shape=[jax.ShapeDtypeStruct((B, LRU_WIDTH // LANES, TP, LANES), F32),
                   row(LRU_WIDTH, BF16),
                   jax.ShapeDtypeStruct((B, ATT_WIDTH, TP), BF16),
                   row(ATT_WIDTH, BF16), row(ATT_WIDTH, BF16)],
        compiler_params=_params("parallel", "arbitrary"),
        name="in_proj",
    )(h, g, w, cos, sin)


def _lru_kernel(xr_ref, gr_ref, keep_ref, cw_ref, cb_ref, wa_ref, ba_ref, wx_ref, bx_ref,
                ap_ref, gn_ref, out_ref, a_s, b_s, h_s, tails, carry):
    n_slabs, tt, _ = xr_ref.shape
    seg = tt // SUBLANES
    sub = lax.broadcasted_iota(jnp.int32, (SUBLANES, LRU_WIDTH), 0)

    @pl.when(pl.program_id(1) == 0)
    def _():
        tails[...] = jnp.zeros_like(tails)
        carry[...] = jnp.zeros_like(carry)

    def permuted(j):
        return jnp.concatenate([xr_ref[c, pl.ds(j, SUBLANES, stride=seg), :]
                                for c in range(n_slabs)], axis=1)

    xp = [permuted(j) for j in range(seg)]
    taps = LRU_CONV - 1
    heads = [jnp.where(sub == 0,
                       pltpu.roll(tails[k * SUBLANES:(k + 1) * SUBLANES, :], 1, 0),
                       pltpu.roll(xp[seg - taps + k], 1, 0)) for k in range(taps)]
    for k in range(taps):
        tails[k * SUBLANES:(k + 1) * SUBLANES, :] = xp[seg - taps + k]
    ext = jnp.concatenate(heads + xp, axis=0)

    xc = cb_ref[...]
    for k in range(LRU_CONV):
        xc = xc + ext[k * SUBLANES:k * SUBLANES + tt, :] * cw_ref[k:k + 1, :]

    xcb = xc.astype(BF16)

    def gate(w_ref, b_ref):
        parts = [_dot(xcb[:, j * GATE_TILE:(j + 1) * GATE_TILE], w_ref[j])
                 for j in range(LRU_WIDTH // GATE_TILE)]
        return _sigmoid(jnp.concatenate(parts, axis=1) + b_ref[...])

    r = gate(wa_ref, ba_ref)
    gi = gate(wx_ref, bx_ref)
    ap = ap_ref[...]
    log_sig = jnp.minimum(ap, 0.0) - jnp.log1p(jnp.exp(-jnp.abs(ap)))
    a = jnp.exp(LRU_C * r * log_sig)
    a_s[...] = a
    d = 1.0 - a * a
    b = jnp.where(d > 0.0, d * lax.rsqrt(d), 0.0) * (gi * xc)
    keep = keep_ref[...]
    b_s[...] = b * jnp.concatenate([keep] * n_slabs, axis=1)

    rows = lambda ref, j: ref[j * SUBLANES:(j + 1) * SUBLANES, :]

    h_end = jnp.zeros((SUBLANES, LRU_WIDTH), F32)
    a_prod = jnp.ones((SUBLANES, LRU_WIDTH), F32)
    for j in range(seg):
        aj = rows(a_s, j)
        h_end = aj * h_end + rows(b_s, j)
        a_prod = aj * a_prod

    c_in = carry[...]
    enter = c_in
    for _ in range(SUBLANES - 1):
        enter = jnp.where(sub == 0, c_in, pltpu.roll(h_end + a_prod * enter, 1, 0))
    last = h_end + a_prod * enter
    carry[...] = jnp.broadcast_to(last[SUBLANES - 1:SUBLANES, :], (SUBLANES, LRU_WIDTH))

    hv = enter
    for j in range(seg):
        hv = rows(a_s, j) * hv + rows(b_s, j)
        for c in range(n_slabs):
            h_s[c, pl.ds(j, SUBLANES, stride=seg), :] = hv[:, c * LANES:(c + 1) * LANES]

    hn = jnp.concatenate([h_s[c] for c in range(n_slabs)], axis=1)
    y = hn * _gelu(gr_ref[...].astype(F32))
    out_ref[...] = (_rms(y, NORM_EPS) * gn_ref[...]).astype(BF16)


def _lru(xr, gr, keep, cw, cb, wa, ba, wx, bx, ap, gn, tt):
    B, n_slabs, TP, _ = xr.shape
    rows = pl.BlockSpec((None, tt, LRU_WIDTH), lambda b, i: (b, i, 0))
    vec = _full((1, LRU_WIDTH))
    wspec = _full((LRU_WIDTH // GATE_TILE, GATE_TILE, GATE_TILE))
    taps = LRU_CONV - 1
    return pl.pallas_call(
        _lru_kernel,
        grid=(B, TP // tt),
        in_specs=[pl.BlockSpec((None, n_slabs, tt, LANES), lambda b, i: (b, 0, i, 0)), rows,
                  pl.BlockSpec((None, tt, LANES), lambda b, i: (jnp.minimum(i, 1), 0, 0)),
                  _full((LRU_CONV, LRU_WIDTH)), vec, wspec, vec, wspec, vec, vec, vec],
        out_specs=rows,
        out_shape=jax.ShapeDtypeStruct((B, TP, LRU_WIDTH), BF16),
        scratch_shapes=[pltpu.VMEM((tt, LRU_WIDTH), F32),
                        pltpu.VMEM((tt, LRU_WIDTH), F32),
                        pltpu.VMEM((n_slabs, tt, LANES), F32),
                        pltpu.VMEM((taps * SUBLANES, LRU_WIDTH), F32),
                        pltpu.VMEM((SUBLANES, LRU_WIDTH), F32)],
        compiler_params=_params("parallel", "arbitrary"),
        name="lru",
    )(xr, gr, keep, cw, cb, wa, ba, wx, bx, ap, gn)


def _attn_kernel(qt_ref, k_ref, v_ref, lam_ref, sub_ref, out_ref,
                 qc_s, s0, s1, m_s, l_s, acc_s, *, lambda_init, tq):
    n_tiles = (qt_ref.shape[1] - FRONT) // tq
    lp = lam_ref[...]
    lam = (jnp.exp(jnp.sum(lp[0:1] * lp[1:2], axis=1, keepdims=True))
           - jnp.exp(jnp.sum(lp[2:3] * lp[3:4], axis=1, keepdims=True)) + lambda_init)
    gain = sub_ref[...] * (1.0 - lambda_init)
    k_meta = k_ref[PAD:FRONT, :]
    v_meta = v_ref[PAD:FRONT, :]

    def qcat(qt):
        dim = lax.broadcasted_iota(jnp.int32, qt.shape, 0)
        zero = jnp.zeros_like(qt)
        return jnp.concatenate([jnp.where(dim < HEAD_DIM, qt, zero),
                                jnp.where(dim >= HEAD_DIM, qt, zero)], axis=1)

    def meta_state(qc):
        s = _dot(k_meta, qc)
        m = jnp.max(s, axis=0, keepdims=True)
        p = jnp.exp2(s - m)
        return m, jnp.sum(p, axis=0, keepdims=True), _dot_t(v_meta, p.astype(BF16))

    def finish(acc, l, rows):
        n = acc.shape[1] // 2
        o = acc * (1.0 / l)
        o = o[:, :n] - lam * o[:, n:]
        o = o * lax.rsqrt(jnp.mean(o * o, axis=0, keepdims=True) + SUBLN_EPS)
        out_ref[rows, :] = (o.T * gain).astype(BF16)

    def scores(s_ref, j):
        start = pl.multiple_of(FRONT + j * tq, LANES)
        s_ref[...] = _dot(k_ref[pl.ds(start, tq), :], qc_s[...])

    def absorb(s_ref, j, masked):
        s = s_ref[...]
        if masked:
            kc = lax.broadcasted_iota(jnp.int32, (tq, 1), 0) // CHUNK
            qc_ = (lax.broadcasted_iota(jnp.int32, (1, 2 * tq), 1) % tq) // CHUNK
            s = jnp.where(kc <= qc_, s, NEG_INF)
        m_old = m_s[...]
        m_new = jnp.maximum(m_old, jnp.max(s, axis=0, keepdims=True))
        alpha = jnp.exp2(m_old - m_new)
        p = jnp.exp2(s - m_new)
        l_s[...] = alpha * l_s[...] + jnp.sum(p, axis=0, keepdims=True)
        m_s[...] = m_new
        vb = v_ref[pl.ds(pl.multiple_of(FRONT + j * tq, LANES), tq), :]
        acc_s[...] = alpha * acc_s[...] + _dot_t(vb, p.astype(BF16))

    m, l, acc = meta_state(qcat(qt_ref[:, 0:FRONT]))
    finish(acc, l, slice(0, FRONT))
    out_ref[0:PAD, :] = jnp.zeros((PAD, V_DIM), BF16)

    def tile(i, c):
        start = pl.multiple_of(FRONT + i * tq, LANES)
        qc_s[...] = qcat(qt_ref[:, pl.ds(start, tq)])
        m, l, acc = meta_state(qc_s[...])
        m_s[...] = m
        l_s[...] = l
        acc_s[...] = acc
        scores(s0, 0)

        def pair(jj, c2):
            j = 2 * jj
            scores(s1, j + 1)
            absorb(s0, j, False)
            scores(s0, j + 2)
            absorb(s1, j + 1, False)
            return c2

        lax.fori_loop(0, i // 2, pair, 0)

        @pl.when(i % 2 == 1)
        def _():
            scores(s1, i)
            absorb(s0, i - 1, False)
            absorb(s1, i, True)

        @pl.when(i % 2 == 0)
        def _():
            absorb(s0, i, True)

        finish(acc_s[...], l_s[...], pl.ds(start, tq))
        return c

    lax.fori_loop(0, n_tiles, tile, 0)


def _attn(qt, k, v, lam, sub, lambda_init, tq):
    B, TP, _ = k.shape
    kern = functools.partial(_attn_kernel, lambda_init=lambda_init, tq=tq)
    return pl.pallas_call(
        kern,
        grid=(B, HEADS),
        in_specs=[pl.BlockSpec((None, V_DIM, TP), lambda b, h: (b, h, 0)),
                  pl.BlockSpec((None, TP, V_DIM), lambda b, h: (b, 0, h)),
                  pl.BlockSpec((None, TP, V_DIM), lambda b, h: (b, 0, h)),
                  _full((4, HEAD_DIM)), _full((1, V_DIM))],
        out_specs=pl.BlockSpec((None, TP, V_DIM), lambda b, h: (b, 0, h)),
        out_shape=jax.ShapeDtypeStruct((B, TP, ATT_WIDTH), BF16),
        scratch_shapes=[pltpu.VMEM((V_DIM, 2 * tq), BF16),
                        pltpu.VMEM((tq, 2 * tq), F32), pltpu.VMEM((tq, 2 * tq), F32),
                        pltpu.VMEM((1, 2 * tq), F32), pltpu.VMEM((1, 2 * tq), F32),
                        pltpu.VMEM((V_DIM, 2 * tq), F32)],
        compiler_params=_params("parallel", "parallel"),
        name="diff_attn",
    )(qt, k, v, lam, sub)


def _out_proj_kernel(h_ref, yr_ref, ya_ref, w_ref, g_ref, out_ref):
    y = _dot(yr_ref[...], w_ref[0:LRU_WIDTH, :]) + _dot(ya_ref[...], w_ref[LRU_WIDTH:, :])
    out_ref[...] = h_ref[...] + _rms(y, NORM_EPS) * g_ref[...]


def _out_proj(h, yr, ya, w, g, tm):
    B, TP, _ = h.shape
    rows = lambda width: pl.BlockSpec((None, tm, width), lambda b, i: (b, i, 0))
    return pl.pallas_call(
        _out_proj_kernel,
        grid=(B, TP // tm),
        in_specs=[rows(D_MODEL), rows(LRU_WIDTH), rows(ATT_WIDTH),
                  _full((D_MODEL, D_MODEL)), _full((1, D_MODEL))],
        out_specs=rows(D_MODEL),
        out_shape=jax.ShapeDtypeStruct(h.shape, F32),
        compiler_params=_params("parallel", "parallel"),
        name="out_proj",
    )(h, yr, ya, w, g)


def _ffn_kernel(h_ref, gpre_ref, wg_ref, wv_ref, cw_ref, cb_ref, wd_ref, gpost_ref,
                out_ref, uext, g_a, g_b, v_a, v_b, acc_s, *, tf):
    tm = h_ref.shape[0]
    halo = BF16_ROWS
    first = pl.program_id(1) == 0

    @pl.when(first)
    def _():
        uext[0:halo, :] = jnp.zeros((halo, D_MODEL), BF16)

    @pl.when(jnp.logical_not(first))
    def _():
        uext[0:halo, :] = uext[tm:tm + halo, :]

    h = h_ref[...]
    uext[halo:, :] = (_rms(h, NORM_EPS) * gpre_ref[...]).astype(BF16)

    n_chunks = D_FF // tf
    bufs = ((g_a, v_a), (g_b, v_b))

    def up(c):
        g_s, v_s = bufs[c % 2]
        cols = slice(c * tf, (c + 1) * tf)
        g_s[...] = _dot(uext[...], wg_ref[:, cols])
        v_s[...] = _dot(uext[halo:, :], wv_ref[:, cols])

    up(0)
    for c in range(n_chunks):
        cols = slice(c * tf, (c + 1) * tf)
        g_s, v_s = bufs[c % 2]
        if c + 1 < n_chunks:
            up(c + 1)
        gc = cb_ref[:, cols]
        for k in range(FFN_CONV):
            off = halo - (FFN_CONV - 1) + k
            gc = gc + g_s[off:off + tm, :] * cw_ref[k:k + 1, cols]
        act = (_gelu(gc) * v_s[...]).astype(BF16)
        part = _dot(act, wd_ref[cols, :])
        if c == 0:
            acc_s[...] = part
        else:
            acc_s[...] += part

    out_ref[...] = h + _rms(acc_s[...], NORM_EPS) * gpost_ref[...]


def _ffn(h, gpre, wg, wv, cw, cb, wd, gpost, tm, tf):
    B, TP, _ = h.shape
    rows = pl.BlockSpec((None, tm, D_MODEL), lambda b, i: (b, i, 0))
    once = pl.Buffered(1)
    const = lambda shape: pl.BlockSpec(shape, lambda b, i: (0, 0), pipeline_mode=once)
    return pl.pallas_call(
        functools.partial(_ffn_kernel, tf=tf),
        grid=(B, TP // tm),
        in_specs=[rows, _full((1, D_MODEL)), const((D_MODEL, D_FF)), const((D_MODEL, D_FF)),
                  _full((FFN_CONV, D_FF)), _full((1, D_FF)), const((D_FF, D_MODEL)),
                  _full((1, D_MODEL))],
        out_specs=rows,
        out_shape=jax.ShapeDtypeStruct(h.shape, F32),
        scratch_shapes=[pltpu.VMEM((tm + BF16_ROWS, D_MODEL), BF16),
                        pltpu.VMEM((tm + BF16_ROWS, tf), F32),
                        pltpu.VMEM((tm + BF16_ROWS, tf), F32),
                        pltpu.VMEM((tm, tf), F32), pltpu.VMEM((tm, tf), F32),
                        pltpu.VMEM((tm, D_MODEL), F32)],
        compiler_params=_params("parallel", "arbitrary"),
        name="ffn",
    )(h, gpre, wg, wv, cw, cb, wd, gpost)


def _block_diag(w):
    per = GATE_TILE // (LRU_WIDTH // LRU_BLOCKS)
    bd = w.shape[-1]
    w = w.reshape(LRU_WIDTH // GATE_TILE, per, bd, bd)
    eye = jnp.eye(per, dtype=w.dtype)
    out = jnp.einsum('jpcd,pq->jpcqd', w, eye)
    return out.reshape(LRU_WIDTH // GATE_TILE, GATE_TILE, GATE_TILE).astype(BF16)


def _keep_rows(tt):
    p = jnp.arange(tt)
    orig = (p % SUBLANES) * (tt // SUBLANES) + p // SUBLANES
    keep = jnp.stack([(orig >= PAD).astype(F32), jnp.ones(tt, F32)])
    return jnp.broadcast_to(keep[:, :, None], (2, tt, LANES))


def _rope_tables(seq):
    inv = 1.0 / (ROPE_THETA ** (jnp.arange(0, HEAD_DIM, 2, dtype=F32) / HEAD_DIM))
    pos = jnp.concatenate([jnp.zeros(PAD, jnp.int32), jnp.arange(N_META + seq)]).astype(F32)
    ang = pos[:, None] * inv[None, :]
    cos, sin = jnp.cos(ang), jnp.sin(ang)
    reps = LANES // HEAD_DIM
    return (jnp.tile(jnp.concatenate([cos, cos], axis=1), (1, reps)),
            jnp.tile(jnp.concatenate([-sin, sin], axis=1), (1, reps)))


def kernel(x, meta_tokens, ln_mix_pre, ln_mix_post, ln_ffn_pre, ln_ffn_post, w_in, lru_conv_w, lru_conv_b, lru_wa, lru_ba, lru_wx, lru_bx, lru_a_param, lru_out_norm, lam_q1, lam_k1, lam_q2, lam_k2, diff_subln, w_out, w_up, ffn_conv_w, ffn_conv_b, w_down):
    B, S, _ = x.shape
    depth = w_in.shape[0]
    TP = FRONT + S
    assert TP % 384 == 0 and TP % 528 == 0 and TP % 1056 == 0 and S % 512 == 0
    meta = jnp.broadcast_to(meta_tokens[None].astype(x.dtype), (B, N_META, D_MODEL))
    h = jnp.concatenate([jnp.zeros((B, PAD, D_MODEL), x.dtype), meta, x], axis=1)
    cos, sin = _rope_tables(S)
    vec = lambda a: a.reshape(1, -1)

    for l in range(depth):
        lambda_init = 0.8 - 0.6 * math.exp(-0.3 * l)
        xr, gr, qt, k, v = _in_proj(h, vec(ln_mix_pre[l]), w_in[l].astype(BF16), cos, sin, tm=384)
        yr = _lru(xr, gr, _keep_rows(528), lru_conv_w[l], vec(lru_conv_b[l]),
                  _block_diag(lru_wa[l]), vec(lru_ba[l]), _block_diag(lru_wx[l]), vec(lru_bx[l]),
                  vec(lru_a_param[l]), vec(lru_out_norm[l]), tt=528)
        lam = jnp.stack([lam_q1[l], lam_k1[l], lam_q2[l], lam_k2[l]])
        ya = _attn(qt, k, v, lam, vec(diff_subln[l]), lambda_init, tq=512)
        h = _out_proj(h, yr, ya, w_out[l].astype(BF16), vec(ln_mix_post[l]), tm=1056)
        wu = w_up[l].astype(BF16)
        h = _ffn(h, vec(ln_ffn_pre[l]), wu[:, :D_FF], wu[:, D_FF:], ffn_conv_w[l],
                 vec(ffn_conv_b[l]), w_down[l].astype(BF16), vec(ln_ffn_post[l]), tm=528, tf=512)
    return h[:, FRONT:]
```

```python
import functools
import math

import jax
import jax.numpy as jnp
from jax import lax
from jax.experimental import pallas as pl
from jax.experimental.pallas import tpu as pltpu

D_MODEL = 1024
N_META = 16
CHUNK = 64
LRU_WIDTH = 512
LRU_BLOCKS = 8
LRU_CONV = 4
LRU_C = 8.0
ATT_WIDTH = 512
HEADS = 4
HEAD_DIM = 64
V_DIM = 128
ROPE_THETA = 10000.0
D_FF = 3 * D_MODEL
FFN_CONV = 3
NORM_EPS = 1e-6
SUBLN_EPS = 1e-5
IN_WIDTH = 2 * LRU_WIDTH + 3 * ATT_WIDTH
NEG_INF = -1e30

LANES = 128
SUBLANES = 8
BF16_ROWS = 16
GATE_TILE = 256
FRONT = LANES
PAD = FRONT - N_META

VMEM_LIMIT = 56 * 1024 * 1024

F32 = jnp.float32
BF16 = jnp.bfloat16


def _dot(a, b):
    return jnp.dot(a, b, preferred_element_type=F32)


def _dot_t(a, b):
    return lax.dot_general(a, b, (((0,), (0,)), ((), ())), preferred_element_type=F32)


def _gelu(x):
    c = math.sqrt(2.0 / math.pi)
    return 0.5 * x * (1.0 + jnp.tanh(c * (x + 0.044715 * (x * x * x))))


def _sigmoid(x):
    return 1.0 / (1.0 + jnp.exp(-x))


def _rms(x, eps):
    return x * lax.rsqrt(jnp.mean(x * x, axis=-1, keepdims=True) + eps)


def _params(*sem):
    return pltpu.CompilerParams(dimension_semantics=sem, vmem_limit_bytes=VMEM_LIMIT)


def _full(shape):
    n = len(shape)
    return pl.BlockSpec(shape, lambda *_: (0,) * n)


def _in_proj_kernel(h_ref, g_ref, w_ref, cos_ref, sin_ref,
                    xr_ref, gr_ref, qt_ref, k_ref, v_ref):
    u = (_rms(h_ref[...], NORM_EPS) * g_ref[...]).astype(BF16)
    zx = _dot(u, w_ref[:, 0:LRU_WIDTH])
    for c in range(LRU_WIDTH // LANES):
        xr_ref[c] = zx[:, c * LANES:(c + 1) * LANES]
    gr_ref[...] = _dot(u, w_ref[:, LRU_WIDTH:2 * LRU_WIDTH]).astype(BF16)
    o = 2 * LRU_WIDTH
    zq = _dot(u, w_ref[:, o:o + ATT_WIDTH])
    zk = _dot(u, w_ref[:, o + ATT_WIDTH:o + 2 * ATT_WIDTH])
    v_ref[...] = _dot(u, w_ref[:, o + 2 * ATT_WIDTH:o + 3 * ATT_WIDTH]).astype(BF16)

    cos = cos_ref[...]
    sin = sin_ref[...]
    tm = cos.shape[0]
    lane = lax.broadcasted_iota(jnp.int32, (tm, LANES), 1)
    low_half = (lane & (HEAD_DIM // 2)) == 0

    def rope(x):
        up = pltpu.roll(x, LANES - HEAD_DIM // 2, 1)
        down = pltpu.roll(x, HEAD_DIM // 2, 1)
        return x * cos + jnp.where(low_half, up, down) * sin

    q_scale = HEAD_DIM ** -0.5 * math.log2(math.e)
    for hd in range(HEADS):
        sl = slice(hd * LANES, (hd + 1) * LANES)
        qh = rope(zq[:, sl]) * q_scale
        qt_ref[sl, :] = qh.T.astype(BF16)
        k_ref[:, sl] = rope(zk[:, sl]).astype(BF16)


def _in_proj(h, g, w, cos, sin, tm):
    B, TP, _ = h.shape
    row = lambda width, dt: jax.ShapeDtypeStruct((B, TP, width), dt)
    rows = lambda width: pl.BlockSpec((None, tm, width), lambda b, i: (b, i, 0))
    return pl.pallas_call(
        _in_proj_kernel,
        grid=(B, TP // tm),
        in_specs=[rows(D_MODEL), _full((1, D_MODEL)), _full((D_MODEL, IN_WIDTH)),
                  pl.BlockSpec((tm, LANES), lambda b, i: (i, 0)),
                  pl.BlockSpec((tm, LANES), lambda b, i: (i, 0))],
        out_specs=[pl.BlockSpec((None, LRU_WIDTH // LANES, tm, LANES), lambda b, i: (b, 0, i, 0)),
                   rows(LRU_WIDTH),
                   pl.BlockSpec((None, ATT_WIDTH, tm), lambda b, i: (b, 0, i)),
                   rows(ATT_WIDTH), rows(ATT_WIDTH)],
        out_shape=[jax.ShapeDtypeStruct((B, LRU_WIDTH // LANES, TP, LANES), F32),
                   row(LRU_WIDTH, BF16),
                   jax.ShapeDtypeStruct((B, ATT_WIDTH, TP), BF16),
                   row(ATT_WIDTH, BF16), row(ATT_WIDTH, BF16)],
        compiler_params=_params("parallel", "arbitrary"),
        name="in_proj",
    )(h, g, w, cos, sin)


def _lru_kernel(xr_ref, gr_ref, keep_ref, cw_ref, cb_ref, wa_ref, ba_ref, wx_ref, bx_ref,
                ap_ref, gn_ref, out_ref, a_s, b_s, h_s, tails, carry):
    n_slabs, tt, _ = xr_ref.shape
    seg = tt // SUBLANES
    sub = lax.broadcasted_iota(jnp.int32, (SUBLANES, LRU_WIDTH), 0)

    @pl.when(pl.program_id(1) == 0)
    def _():
        tails[...] = jnp.zeros_like(tails)
        carry[...] = jnp.zeros_like(carry)

    def permuted(j):
        return jnp.concatenate([xr_ref[c, pl.ds(j, SUBLANES, stride=seg), :]
                                for c in range(n_slabs)], axis=1)

    xp = [permuted(j) for j in range(seg)]
    taps = LRU_CONV - 1
    heads = [jnp.where(sub == 0,
                       pltpu.roll(tails[k * SUBLANES:(k + 1) * SUBLANES, :], 1, 0),
                       pltpu.roll(xp[seg - taps + k], 1, 0)) for k in range(taps)]
    for k in range(taps):
        tails[k * SUBLANES:(k + 1) * SUBLANES, :] = xp[seg - taps + k]
    ext = jnp.concatenate(heads + xp, axis=0)

    xc = cb_ref[...]
    for k in range(LRU_CONV):
        xc = xc + ext[k * SUBLANES:k * SUBLANES + tt, :] * cw_ref[k:k + 1, :]

    xcb = xc.astype(BF16)

    def gate(w_ref, b_ref):
        parts = [_dot(xcb[:, j * GATE_TILE:(j + 1) * GATE_TILE], w_ref[j])
                 for j in range(LRU_WIDTH // GATE_TILE)]
        return _sigmoid(jnp.concatenate(parts, axis=1) + b_ref[...])

    r = gate(wa_ref, ba_ref)
    gi = gate(wx_ref, bx_ref)
    ap = ap_ref[...]
    log_sig = jnp.minimum(ap, 0.0) - jnp.log1p(jnp.exp(-jnp.abs(ap)))
    a = jnp.exp(LRU_C * r * log_sig)
    a_s[...] = a
    d = 1.0 - a * a
    b = jnp.where(d > 0.0, d * lax.rsqrt(d), 0.0) * (gi * xc)
    keep = keep_ref[...]
    b_s[...] = b * jnp.concatenate([keep] * n_slabs, axis=1)

    rows = lambda ref, j: ref[j * SUBLANES:(j + 1) * SUBLANES, :]

    h_end = jnp.zeros((SUBLANES, LRU_WIDTH), F32)
    a_prod = jnp.ones((SUBLANES, LRU_WIDTH), F32)
    for j in range(seg):
        aj = rows(a_s, j)
        h_end = aj * h_end + rows(b_s, j)
        a_prod = aj * a_prod

    c_in = carry[...]
    enter = c_in
    for _ in range(SUBLANES - 1):
        enter = jnp.where(sub == 0, c_in, pltpu.roll(h_end + a_prod * enter, 1, 0))
    last = h_end + a_prod * enter
    carry[...] = jnp.broadcast_to(last[SUBLANES - 1:SUBLANES, :], (SUBLANES, LRU_WIDTH))

    hv = enter
    for j in range(seg):
        hv = rows(a_s, j) * hv + rows(b_s, j)
        for c in range(n_slabs):
            h_s[c, pl.ds(j, SUBLANES, stride=seg), :] = hv[:, c * LANES:(c + 1) * LANES]

    hn = jnp.concatenate([h_s[c] for c in range(n_slabs)], axis=1)
    y = hn * _gelu(gr_ref[...].astype(F32))
    out_ref[...] = (_rms(y, NORM_EPS) * gn_ref[...]).astype(BF16)


def _lru(xr, gr, keep, cw, cb, wa, ba, wx, bx, ap, gn, tt):
    B, n_slabs, TP, _ = xr.shape
    rows = pl.BlockSpec((None, tt, LRU_WIDTH), lambda b, i: (b, i, 0))
    vec = _full((1, LRU_WIDTH))
    wspec = _full((LRU_WIDTH // GATE_TILE, GATE_TILE, GATE_TILE))
    taps = LRU_CONV - 1
    return pl.pallas_call(
        _lru_kernel,
        grid=(B, TP // tt),
        in_specs=[pl.BlockSpec((None, n_slabs, tt, LANES), lambda b, i: (b, 0, i, 0)), rows,
                  pl.BlockSpec((None, tt, LANES), lambda b, i: (jnp.minimum(i, 1), 0, 0)),
                  _full((LRU_CONV, LRU_WIDTH)), vec, wspec, vec, wspec, vec, vec, vec],
        out_specs=rows,
        out_shape=jax.ShapeDtypeStruct((B, TP, LRU_WIDTH), BF16),
        scratch_shapes=[pltpu.VMEM((tt, LRU_WIDTH), F32),
                        pltpu.VMEM((tt, LRU_WIDTH), F32),
                        pltpu.VMEM((n_slabs, tt, LANES), F32),
                        pltpu.VMEM((taps * SUBLANES, LRU_WIDTH), F32),
                        pltpu.VMEM((SUBLANES, LRU_WIDTH), F32)],
        compiler_params=_params("parallel", "arbitrary"),
        name="lru",
    )(xr, gr, keep, cw, cb, wa, ba, wx, bx, ap, gn)


def _attn_kernel(qt_ref, k_ref, v_ref, lam_ref, sub_ref, out_ref,
                 qc_s, s0, s1, m_s, l_s, acc_s, *, lambda_init, tq):
    n_tiles = (qt_ref.shape[1] - FRONT) // tq
    lp = lam_ref[...]
    lam = (jnp.exp(jnp.sum(lp[0:1] * lp[1:2], axis=1, keepdims=True))
           - jnp.exp(jnp.sum(lp[2:3] * lp[3:4], axis=1, keepdims=True)) + lambda_init)
    gain = sub_ref[...] * (1.0 - lambda_init)
    k_meta = k_ref[PAD:FRONT, :]
    v_meta = v_ref[PAD:FRONT, :]

    def qcat(qt):
        dim = lax.broadcasted_iota(jnp.int32, qt.shape, 0)
        zero = jnp.zeros_like(qt)
        return jnp.concatenate([jnp.where(dim < HEAD_DIM, qt, zero),
                                jnp.where(dim >= HEAD_DIM, qt, zero)], axis=1)

    def meta_state(qc):
        s = _dot(k_meta, qc)
        m = jnp.max(s, axis=0, keepdims=True)
        p = jnp.exp2(s - m)
        return m, jnp.sum(p, axis=0, keepdims=True), _dot_t(v_meta, p.astype(BF16))

    def finish(acc, l, rows):
        n = acc.shape[1] // 2
        o = acc * (1.0 / l)
        o = o[:, :n] - lam * o[:, n:]
        o = o * lax.rsqrt(jnp.mean(o * o, axis=0, keepdims=True) + SUBLN_EPS)
        out_ref[rows, :] = (o.T * gain).astype(BF16)

    def scores(s_ref, j):
        start = pl.multiple_of(FRONT + j * tq, LANES)
        s_ref[...] = _dot(k_ref[pl.ds(start, tq), :], qc_s[...])

    def absorb(s_ref, j, masked):
        s = s_ref[...]
        if masked:
            kc = lax.broadcasted_iota(jnp.int32, (tq, 1), 0) // CHUNK
            qc_ = (lax.broadcasted_iota(jnp.int32, (1, 2 * tq), 1) % tq) // CHUNK
            s = jnp.where(kc <= qc_, s, NEG_INF)
        m_old = m_s[...]
        m_new = jnp.maximum(m_old, jnp.max(s, axis=0, keepdims=True))
        alpha = jnp.exp2(m_old - m_new)
        p = jnp.exp2(s - m_new)
        l_s[...] = alpha * l_s[...] + jnp.sum(p, axis=0, keepdims=True)
        m_s[...] = m_new
        vb = v_ref[pl.ds(pl.multiple_of(FRONT + j * tq, LANES), tq), :]
        acc_s[...] = alpha * acc_s[...] + _dot_t(vb, p.astype(BF16))

    m, l, acc = meta_state(qcat(qt_ref[:, 0:FRONT]))
    finish(acc, l, slice(0, FRONT))
    out_ref[0:PAD, :] = jnp.zeros((PAD, V_DIM), BF16)

    def tile(i, c):
        start = pl.multiple_of(FRONT + i * tq, LANES)
        qc_s[...] = qcat(qt_ref[:, pl.ds(start, tq)])
        m, l, acc = meta_state(qc_s[...])
        m_s[...] = m
        l_s[...] = l
        acc_s[...] = acc
        scores(s0, 0)

        def pair(jj, c2):
            j = 2 * jj
            scores(s1, j + 1)
            absorb(s0, j, False)
            scores(s0, j + 2)
            absorb(s1, j + 1, False)
            return c2

        lax.fori_loop(0, i // 2, pair, 0)

        @pl.when(i % 2 == 1)
        def _():
            scores(s1, i)
            absorb(s0, i - 1, False)
            absorb(s1, i, True)

        @pl.when(i % 2 == 0)
        def _():
            absorb(s0, i, True)

        finish(acc_s[...], l_s[...], pl.ds(start, tq))
        return c

    lax.fori_loop(0, n_tiles, tile, 0)


def _attn(qt, k, v, lam, sub, lambda_init, tq):
    B, TP, _ = k.shape
    kern = functools.partial(_attn_kernel, lambda_init=lambda_init, tq=tq)
    return pl.pallas_call(
        kern,
        grid=(B, HEADS),
        in_specs=[pl.BlockSpec((None, V_DIM, TP), lambda b, h: (b, h, 0)),
                  pl.BlockSpec((None, TP, V_DIM), lambda b, h: (b, 0, h)),
                  pl.BlockSpec((None, TP, V_DIM), lambda b, h: (b, 0, h)),
                  _full((4, HEAD_DIM)), _full((1, V_DIM))],
        out_specs=pl.BlockSpec((None, TP, V_DIM), lambda b, h: (b, 0, h)),
        out_shape=jax.ShapeDtypeStruct((B, TP, ATT_WIDTH), BF16),
        scratch_shapes=[pltpu.VMEM((V_DIM, 2 * tq), BF16),
                        pltpu.VMEM((tq, 2 * tq), F32), pltpu.VMEM((tq, 2 * tq), F32),
                        pltpu.VMEM((1, 2 * tq), F32), pltpu.VMEM((1, 2 * tq), F32),
                        pltpu.VMEM((V_DIM, 2 * tq), F32)],
        compiler_params=_params("parallel", "parallel"),
        name="diff_attn",
    )(qt, k, v, lam, sub)


def _mix_ffn_kernel(h_ref, yr_ref, ya_ref, wo_ref, gmix_ref, gpre_ref, wg_ref, wv_ref,
                    cw_ref, cb_ref, wd_ref, gpost_ref, out_ref,
                    uext, hmid, g_a, g_b, v_a, v_b, acc_s, *, tf, n_sub):
    tm = h_ref.shape[0]
    halo = BF16_ROWS
    rs = tm // n_sub
    first = pl.program_id(1) == 0
    n_chunks = D_FF // tf
    bufs = ((g_a, v_a), (g_b, v_b))

    @pl.when(first)
    def _():
        uext[0:halo, :] = jnp.zeros((halo, D_MODEL), BF16)

    @pl.when(jnp.logical_not(first))
    def _():
        uext[0:halo, :] = uext[tm:tm + halo, :]

    ys = [_dot(yr_ref[r * rs:(r + 1) * rs, :], wo_ref[0:LRU_WIDTH, :])
          + _dot(ya_ref[r * rs:(r + 1) * rs, :], wo_ref[LRU_WIDTH:, :]) for r in range(n_sub)]
    for r in range(n_sub):
        rows = slice(r * rs, (r + 1) * rs)
        hm = h_ref[rows, :] + _rms(ys[r], NORM_EPS) * gmix_ref[...]
        hmid[rows, :] = hm
        uext[halo + r * rs:halo + (r + 1) * rs, :] = (_rms(hm, NORM_EPS) * gpre_ref[...]).astype(BF16)

    def up_rows(c, lo, hi):
        g_s, v_s = bufs[c % 2]
        cols = slice(c * tf, (c + 1) * tf)
        glo = 0 if lo == 0 else halo + lo
        g_s[glo:halo + hi, :] = _dot(uext[glo:halo + hi, :], wg_ref[:, cols])
        v_s[lo:hi, :] = _dot(uext[halo + lo:halo + hi, :], wv_ref[:, cols])

    def gate_rows(c, lo, hi):
        g_s, v_s = bufs[c % 2]
        cols = slice(c * tf, (c + 1) * tf)
        gc = cb_ref[:, cols]
        for k in range(FFN_CONV):
            off = halo - (FFN_CONV - 1) + k
            gc = gc + g_s[off + lo:off + hi, :] * cw_ref[k:k + 1, cols]
        return (_gelu(gc) * v_s[lo:hi, :]).astype(BF16)

    for r in range(n_sub):
        up_rows(0, r * rs, (r + 1) * rs)

    for c in range(n_chunks - 1):
        cols = slice(c * tf, (c + 1) * tf)
        up_rows(c + 1, 0, tm)
        part = _dot(gate_rows(c, 0, tm), wd_ref[cols, :])
        if c == 0:
            acc_s[...] = part
        else:
            acc_s[...] += part

    c = n_chunks - 1
    cols = slice(c * tf, (c + 1) * tf)
    for r in range(n_sub):
        rows = slice(r * rs, (r + 1) * rs)
        f = acc_s[rows, :] + _dot(gate_rows(c, r * rs, (r + 1) * rs), wd_ref[cols, :])
        out_ref[rows, :] = hmid[rows, :] + _rms(f, NORM_EPS) * gpost_ref[...]


def _mix_ffn(h, yr, ya, wo, gmix, gpre, wg, wv, cw, cb, wd, gpost, tm, tf, n_sub):
    B, TP, _ = h.shape
    assert D_FF // tf >= 2 and tm % (n_sub * BF16_ROWS) == 0
    rows = lambda width: pl.BlockSpec((None, tm, width), lambda b, i: (b, i, 0))
    once = pl.Buffered(1)
    const = lambda shape: pl.BlockSpec(shape, lambda b, i: (0, 0), pipeline_mode=once)
    vec = _full((1, D_MODEL))
    return pl.pallas_call(
        functools.partial(_mix_ffn_kernel, tf=tf, n_sub=n_sub),
        grid=(B, TP // tm),
        in_specs=[rows(D_MODEL), rows(LRU_WIDTH), rows(ATT_WIDTH), const((D_MODEL, D_MODEL)),
                  vec, vec, const((D_MODEL, D_FF)), const((D_MODEL, D_FF)),
                  _full((FFN_CONV, D_FF)), _full((1, D_FF)), const((D_FF, D_MODEL)), vec],
        out_specs=rows(D_MODEL),
        out_shape=jax.ShapeDtypeStruct(h.shape, F32),
        scratch_shapes=[pltpu.VMEM((tm + BF16_ROWS, D_MODEL), BF16),
                        pltpu.VMEM((tm, D_MODEL), F32),
                        pltpu.VMEM((tm + BF16_ROWS, tf), F32),
                        pltpu.VMEM((tm + BF16_ROWS, tf), F32),
                        pltpu.VMEM((tm, tf), F32), pltpu.VMEM((tm, tf), F32),
                        pltpu.VMEM((tm, D_MODEL), F32)],
        compiler_params=_params("parallel", "arbitrary"),
        name="mix_ffn",
    )(h, yr, ya, wo, gmix, gpre, wg, wv, cw, cb, wd, gpost)


def _block_diag(w):
    per = GATE_TILE // (LRU_WIDTH // LRU_BLOCKS)
    bd = w.shape[-1]
    w = w.reshape(LRU_WIDTH // GATE_TILE, per, bd, bd)
    eye = jnp.eye(per, dtype=w.dtype)
    out = jnp.einsum('jpcd,pq->jpcqd', w, eye)
    return out.reshape(LRU_WIDTH // GATE_TILE, GATE_TILE, GATE_TILE).astype(BF16)


def _keep_rows(tt):
    p = jnp.arange(tt)
    orig = (p % SUBLANES) * (tt // SUBLANES) + p // SUBLANES
    keep = jnp.stack([(orig >= PAD).astype(F32), jnp.ones(tt, F32)])
    return jnp.broadcast_to(keep[:, :, None], (2, tt, LANES))


def _rope_tables(seq):
    inv = 1.0 / (ROPE_THETA ** (jnp.arange(0, HEAD_DIM, 2, dtype=F32) / HEAD_DIM))
    pos = jnp.concatenate([jnp.zeros(PAD, jnp.int32), jnp.arange(N_META + seq)]).astype(F32)
    ang = pos[:, None] * inv[None, :]
    cos, sin = jnp.cos(ang), jnp.sin(ang)
    reps = LANES // HEAD_DIM
    return (jnp.tile(jnp.concatenate([cos, cos], axis=1), (1, reps)),
            jnp.tile(jnp.concatenate([-sin, sin], axis=1), (1, reps)))


def kernel(x, meta_tokens, ln_mix_pre, ln_mix_post, ln_ffn_pre, ln_ffn_post, w_in, lru_conv_w, lru_conv_b, lru_wa, lru_ba, lru_wx, lru_bx, lru_a_param, lru_out_norm, lam_q1, lam_k1, lam_q2, lam_k2, diff_subln, w_out, w_up, ffn_conv_w, ffn_conv_b, w_down):
    B, S, _ = x.shape
    depth = w_in.shape[0]
    TP = FRONT + S
    assert TP % 384 == 0 and TP % 528 == 0 and S % 512 == 0
    meta = jnp.broadcast_to(meta_tokens[None].astype(x.dtype), (B, N_META, D_MODEL))
    h = jnp.concatenate([jnp.zeros((B, PAD, D_MODEL), x.dtype), meta, x], axis=1)
    cos, sin = _rope_tables(S)
    vec = lambda a: a.reshape(1, -1)

    for l in range(depth):
        lambda_init = 0.8 - 0.6 * math.exp(-0.3 * l)
        xr, gr, qt, k, v = _in_proj(h, vec(ln_mix_pre[l]), w_in[l].astype(BF16), cos, sin, tm=384)
        yr = _lru(xr, gr, _keep_rows(528), lru_conv_w[l], vec(lru_conv_b[l]),
                  _block_diag(lru_wa[l]), vec(lru_ba[l]), _block_diag(lru_wx[l]), vec(lru_bx[l]),
                  vec(lru_a_param[l]), vec(lru_out_norm[l]), tt=528)
        lam = jnp.stack([lam_q1[l], lam_k1[l], lam_q2[l], lam_k2[l]])
        ya = _attn(qt, k, v, lam, vec(diff_subln[l]), lambda_init, tq=512)
        wu = w_up[l].astype(BF16)
        h = _mix_ffn(h, yr, ya, w_out[l].astype(BF16), vec(ln_mix_post[l]), vec(ln_ffn_pre[l]),
                     wu[:, :D_FF], wu[:, D_FF:], ffn_conv_w[l], vec(ffn_conv_b[l]),
                     w_down[l].astype(BF16), vec(ln_ffn_post[l]), tm=528, tf=512, n_sub=3)
    return h[:, FRONT:]
```

```python
import functools
import math

import jax
import jax.numpy as jnp
from jax import lax
from jax.experimental import pallas as pl
from jax.experimental.pallas import tpu as pltpu

D_MODEL = 1024
N_META = 16
CHUNK = 64
LRU_WIDTH = 512
LRU_BLOCKS = 8
LRU_CONV = 4
LRU_C = 8.0
ATT_WIDTH = 512
HEADS = 4
HEAD_DIM = 64
V_DIM = 128
ROPE_THETA = 10000.0
D_FF = 3 * D_MODEL
FFN_CONV = 3
NORM_EPS = 1e-6
SUBLN_EPS = 1e-5
IN_WIDTH = 2 * LRU_WIDTH + 3 * ATT_WIDTH
NEG_INF = -1e30

LANES = 128
SUBLANES = 8
BF16_ROWS = 16
GATE_TILE = 256
FRONT = LANES
PAD = FRONT - N_META

VMEM_LIMIT = 56 * 1024 * 1024

F32 = jnp.float32
BF16 = jnp.bfloat16


def _dot(a, b):
    return jnp.dot(a, b, preferred_element_type=F32)


def _dot_t(a, b):
    return lax.dot_general(a, b, (((0,), (0,)), ((), ())), preferred_element_type=F32)


def _gelu(x):
    c = math.sqrt(2.0 / math.pi)
    return 0.5 * x * (1.0 + jnp.tanh(c * (x + 0.044715 * (x * x * x))))


def _sigmoid(x):
    return 1.0 / (1.0 + jnp.exp(-x))


def _rms(x, eps):
    return x * lax.rsqrt(jnp.mean(x * x, axis=-1, keepdims=True) + eps)


def _params(*sem):
    return pltpu.CompilerParams(dimension_semantics=sem, vmem_limit_bytes=VMEM_LIMIT)


def _full(shape):
    n = len(shape)
    return pl.BlockSpec(shape, lambda *_: (0,) * n)


def _in_proj_kernel(h_ref, g_ref, w_ref, cos_ref, sin_ref,
                    xr_ref, gr_ref, qt_ref, k_ref, v_ref):
    u = (_rms(h_ref[...], NORM_EPS) * g_ref[...]).astype(BF16)
    zx = _dot(u, w_ref[:, 0:LRU_WIDTH])
    for c in range(LRU_WIDTH // LANES):
        xr_ref[c] = zx[:, c * LANES:(c + 1) * LANES]
    gr_ref[...] = _dot(u, w_ref[:, LRU_WIDTH:2 * LRU_WIDTH]).astype(BF16)
    o = 2 * LRU_WIDTH
    zq = _dot(u, w_ref[:, o:o + ATT_WIDTH])
    zk = _dot(u, w_ref[:, o + ATT_WIDTH:o + 2 * ATT_WIDTH])
    v_ref[...] = _dot(u, w_ref[:, o + 2 * ATT_WIDTH:o + 3 * ATT_WIDTH]).astype(BF16)

    cos = cos_ref[...]
    sin = sin_ref[...]
    tm = cos.shape[0]
    lane = lax.broadcasted_iota(jnp.int32, (tm, LANES), 1)
    low_half = (lane & (HEAD_DIM // 2)) == 0

    def rope(x):
        up = pltpu.roll(x, LANES - HEAD_DIM // 2, 1)
        down = pltpu.roll(x, HEAD_DIM // 2, 1)
        return x * cos + jnp.where(low_half, up, down) * sin

    q_scale = HEAD_DIM ** -0.5 * math.log2(math.e)
    for hd in range(HEADS):
        sl = slice(hd * LANES, (hd + 1) * LANES)
        qh = rope(zq[:, sl]) * q_scale
        qt_ref[sl, :] = qh.T.astype(BF16)
        k_ref[:, sl] = rope(zk[:, sl]).astype(BF16)


def _in_proj(h, g, w, cos, sin, tm):
    B, TP, _ = h.shape
    row = lambda width, dt: jax.ShapeDtypeStruct((B, TP, width), dt)
    rows = lambda width: pl.BlockSpec((None, tm, width), lambda b, i: (b, i, 0))
    return pl.pallas_call(
        _in_proj_kernel,
        grid=(B, TP // tm),
        in_specs=[rows(D_MODEL), _full((1, D_MODEL)), _full((D_MODEL, IN_WIDTH)),
                  pl.BlockSpec((tm, LANES), lambda b, i: (i, 0)),
                  pl.BlockSpec((tm, LANES), lambda b, i: (i, 0))],
        out_specs=[pl.BlockSpec((None, LRU_WIDTH // LANES, tm, LANES), lambda b, i: (b, 0, i, 0)),
                   rows(LRU_WIDTH),
                   pl.BlockSpec((None, ATT_WIDTH, tm), lambda b, i: (b, 0, i)),
                   rows(ATT_WIDTH), rows(ATT_WIDTH)],
        out_shape=[jax.ShapeDtypeStruct((B, LRU_WIDTH // LANES, TP, LANES), F32),
                   row(LRU_WIDTH, BF16),
                   jax.ShapeDtypeStruct((B, ATT_WIDTH, TP), BF16),
                   row(ATT_WIDTH, BF16), row(ATT_WIDTH, BF16)],
        compiler_params=_params("parallel", "arbitrary"),
        name="in_proj",
    )(h, g, w, cos, sin)


def _lru_kernel(xr_ref, gr_ref, keep_ref, cw_ref, cb_ref, wa_ref, ba_ref, wx_ref, bx_ref,
                ap_ref, gn_ref, out_ref, a_s, b_s, h_s, tails, carry):
    n_slabs, tt, _ = xr_ref.shape
    seg = tt // SUBLANES
    sub = lax.broadcasted_iota(jnp.int32, (SUBLANES, LRU_WIDTH), 0)

    @pl.when(pl.program_id(1) == 0)
    def _():
        tails[...] = jnp.zeros_like(tails)
        carry[...] = jnp.zeros_like(carry)

    def permuted(j):
        return jnp.concatenate([xr_ref[c, pl.ds(j, SUBLANES, stride=seg), :]
                                for c in range(n_slabs)], axis=1)

    xp = [permuted(j) for j in range(seg)]
    taps = LRU_CONV - 1
    heads = [jnp.where(sub == 0,
                       pltpu.roll(tails[k * SUBLANES:(k + 1) * SUBLANES, :], 1, 0),
                       pltpu.roll(xp[seg - taps + k], 1, 0)) for k in range(taps)]
    for k in range(taps):
        tails[k * SUBLANES:(k + 1) * SUBLANES, :] = xp[seg - taps + k]
    ext = jnp.concatenate(heads + xp, axis=0)

    xc = cb_ref[...]
    for k in range(LRU_CONV):
        xc = xc + ext[k * SUBLANES:k * SUBLANES + tt, :] * cw_ref[k:k + 1, :]

    xcb = xc.astype(BF16)

    def gate(w_ref, b_ref):
        parts = [_dot(xcb[:, j * GATE_TILE:(j + 1) * GATE_TILE], w_ref[j])
                 for j in range(LRU_WIDTH // GATE_TILE)]
        return _sigmoid(jnp.concatenate(parts, axis=1) + b_ref[...])

    r = gate(wa_ref, ba_ref)
    gi = gate(wx_ref, bx_ref)
    ap = ap_ref[...]
    log_sig = jnp.minimum(ap, 0.0) - jnp.log1p(jnp.exp(-jnp.abs(ap)))
    a = jnp.exp(LRU_C * r * log_sig)
    a_s[...] = a
    d = 1.0 - a * a
    b = jnp.where(d > 0.0, d * lax.rsqrt(d), 0.0) * (gi * xc)
    keep = keep_ref[...]
    b_s[...] = b * jnp.concatenate([keep] * n_slabs, axis=1)

    rows = lambda ref, j: ref[j * SUBLANES:(j + 1) * SUBLANES, :]

    h_end = jnp.zeros((SUBLANES, LRU_WIDTH), F32)
    a_prod = jnp.ones((SUBLANES, LRU_WIDTH), F32)
    for j in range(seg):
        aj = rows(a_s, j)
        h_end = aj * h_end + rows(b_s, j)
        a_prod = aj * a_prod

    c_in = carry[...]
    enter = c_in
    for _ in range(SUBLANES - 1):
        enter = jnp.where(sub == 0, c_in, pltpu.roll(h_end + a_prod * enter, 1, 0))
    last = h_end + a_prod * enter
    carry[...] = jnp.broadcast_to(last[SUBLANES - 1:SUBLANES, :], (SUBLANES, LRU_WIDTH))

    hv = enter
    for j in range(seg):
        hv = rows(a_s, j) * hv + rows(b_s, j)
        for c in range(n_slabs):
            h_s[c, pl.ds(j, SUBLANES, stride=seg), :] = hv[:, c * LANES:(c + 1) * LANES]

    hn = jnp.concatenate([h_s[c] for c in range(n_slabs)], axis=1)
    y = hn * _gelu(gr_ref[...].astype(F32))
    out_ref[...] = (_rms(y, NORM_EPS) * gn_ref[...]).astype(BF16)


def _lru(xr, gr, keep, cw, cb, wa, ba, wx, bx, ap, gn, tt):
    B, n_slabs, TP, _ = xr.shape
    rows = pl.BlockSpec((None, tt, LRU_WIDTH), lambda b, i: (b, i, 0))
    vec = _full((1, LRU_WIDTH))
    wspec = _full((LRU_WIDTH // GATE_TILE, GATE_TILE, GATE_TILE))
    taps = LRU_CONV - 1
    return pl.pallas_call(
        _lru_kernel,
        grid=(B, TP // tt),
        in_specs=[pl.BlockSpec((None, n_slabs, tt, LANES), lambda b, i: (b, 0, i, 0)), rows,
                  pl.BlockSpec((None, tt, LANES), lambda b, i: (jnp.minimum(i, 1), 0, 0)),
                  _full((LRU_CONV, LRU_WIDTH)), vec, wspec, vec, wspec, vec, vec, vec],
        out_specs=rows,
        out_shape=jax.ShapeDtypeStruct((B, TP, LRU_WIDTH), BF16),
        scratch_shapes=[pltpu.VMEM((tt, LRU_WIDTH), F32),
                        pltpu.VMEM((tt, LRU_WIDTH), F32),
                        pltpu.VMEM((n_slabs, tt, LANES), F32),
                        pltpu.VMEM((taps * SUBLANES, LRU_WIDTH), F32),
                        pltpu.VMEM((SUBLANES, LRU_WIDTH), F32)],
        compiler_params=_params("parallel", "arbitrary"),
        name="lru",
    )(xr, gr, keep, cw, cb, wa, ba, wx, bx, ap, gn)


def _attn_kernel(qt_ref, k_ref, v_ref, lam_ref, sub_ref, out_ref,
                 qc_s, s0, s1, x0, x1, m_s, l_s, acc_s, *, lambda_init, tq):
    n_tiles = (qt_ref.shape[1] - FRONT) // tq
    lp = lam_ref[...]
    lam = (jnp.exp(jnp.sum(lp[0:1] * lp[1:2], axis=1, keepdims=True))
           - jnp.exp(jnp.sum(lp[2:3] * lp[3:4], axis=1, keepdims=True)) + lambda_init)
    gain = sub_ref[...] * (1.0 - lambda_init)
    k_meta = k_ref[PAD:FRONT, :]
    v_meta = v_ref[PAD:FRONT, :]

    def qcat(qt):
        dim = lax.broadcasted_iota(jnp.int32, qt.shape, 0)
        zero = jnp.zeros_like(qt)
        return jnp.concatenate([jnp.where(dim < HEAD_DIM, qt, zero),
                                jnp.where(dim >= HEAD_DIM, qt, zero)], axis=1)

    def meta_state(qc):
        s = _dot(k_meta, qc)
        m = jnp.max(s, axis=0, keepdims=True)
        p = jnp.exp2(s - m)
        return m, jnp.sum(p, axis=0, keepdims=True), _dot_t(v_meta, p.astype(BF16))

    def finish(acc, l, rows):
        n = acc.shape[1] // 2
        o = acc * (1.0 / l)
        o = o[:, :n] - lam * o[:, n:]
        o = o * lax.rsqrt(jnp.mean(o * o, axis=0, keepdims=True) + SUBLN_EPS)
        out_ref[rows, :] = (o.T * gain).astype(BF16)

    def scores(s_ref, mx_ref, j):
        start = pl.multiple_of(FRONT + j * tq, LANES)
        kb = jnp.concatenate([k_ref[pl.ds(start, tq), :], k_meta], axis=0)
        s = _dot(kb, qc_s[...])
        s_ref[...] = s
        mx_ref[...] = jnp.max(s[0:tq], axis=0, keepdims=True)

    def absorb(s_ref, mx_ref, j, diagonal):
        vb = v_ref[pl.ds(pl.multiple_of(FRONT + j * tq, LANES), tq), :]
        if diagonal:
            s = s_ref[...]
            row = lax.broadcasted_iota(jnp.int32, (tq + N_META, 1), 0)
            kc = jnp.where(row < tq, row // CHUNK, 0)
            qc_ = (lax.broadcasted_iota(jnp.int32, (1, 2 * tq), 1) % tq) // CHUNK
            s = jnp.where(kc <= qc_, s, NEG_INF)
            mx = jnp.max(s, axis=0, keepdims=True)
            vb = jnp.concatenate([vb, v_meta], axis=0)
        else:
            s = s_ref[0:tq, :]
            mx = mx_ref[...]
        m_old = m_s[...]
        m_new = jnp.maximum(m_old, mx)
        alpha = jnp.exp2(m_old - m_new)
        p = jnp.exp2(s - m_new)
        l_s[...] = alpha * l_s[...] + jnp.sum(p, axis=0, keepdims=True)
        m_s[...] = m_new
        acc_s[...] = alpha * acc_s[...] + _dot_t(vb, p.astype(BF16))

    m, l, acc = meta_state(qcat(qt_ref[:, 0:FRONT]))
    finish(acc, l, slice(0, FRONT))
    out_ref[0:PAD, :] = jnp.zeros((PAD, V_DIM), BF16)

    def load_queries(i):
        qc_s[...] = qcat(qt_ref[:, pl.ds(pl.multiple_of(FRONT + i * tq, LANES), tq)])

    load_queries(0)
    scores(s0, x0, 0)

    def tile(i, c):
        m_s[...] = jnp.full_like(m_s, NEG_INF)
        l_s[...] = jnp.zeros_like(l_s)
        acc_s[...] = jnp.zeros_like(acc_s)

        def pair(jj, c2):
            j = 2 * jj
            scores(s1, x1, j + 1)
            absorb(s0, x0, j, False)
            scores(s0, x0, j + 2)
            absorb(s1, x1, j + 1, False)
            return c2

        lax.fori_loop(0, i // 2, pair, 0)

        @pl.when(i % 2 == 1)
        def _():
            scores(s1, x1, i)
            absorb(s0, x0, i - 1, False)
            absorb(s1, x1, i, True)

        @pl.when(i % 2 == 0)
        def _():
            absorb(s0, x0, i, True)

        acc, l = acc_s[...], l_s[...]
        load_queries(jnp.minimum(i + 1, n_tiles - 1))
        scores(s0, x0, 0)
        finish(acc, l, pl.ds(pl.multiple_of(FRONT + i * tq, LANES), tq))
        return c

    lax.fori_loop(0, n_tiles, tile, 0)


def _attn(qt, k, v, lam, sub, lambda_init, tq):
    B, TP, _ = k.shape
    kern = functools.partial(_attn_kernel, lambda_init=lambda_init, tq=tq)
    return pl.pallas_call(
        kern,
        grid=(B, HEADS),
        in_specs=[pl.BlockSpec((None, V_DIM, TP), lambda b, h: (b, h, 0)),
                  pl.BlockSpec((None, TP, V_DIM), lambda b, h: (b, 0, h)),
                  pl.BlockSpec((None, TP, V_DIM), lambda b, h: (b, 0, h)),
                  _full((4, HEAD_DIM)), _full((1, V_DIM))],
        out_specs=pl.BlockSpec((None, TP, V_DIM), lambda b, h: (b, 0, h)),
        out_shape=jax.ShapeDtypeStruct((B, TP, ATT_WIDTH), BF16),
        scratch_shapes=[pltpu.VMEM((V_DIM, 2 * tq), BF16),
                        pltpu.VMEM((tq + N_META, 2 * tq), F32),
                        pltpu.VMEM((tq + N_META, 2 * tq), F32),
                        pltpu.VMEM((1, 2 * tq), F32), pltpu.VMEM((1, 2 * tq), F32),
                        pltpu.VMEM((1, 2 * tq), F32), pltpu.VMEM((1, 2 * tq), F32),
                        pltpu.VMEM((V_DIM, 2 * tq), F32)],
        compiler_params=_params("parallel", "parallel"),
        name="diff_attn",
    )(qt, k, v, lam, sub)


def _mix_ffn_kernel(h_ref, yr_ref, ya_ref, wo_ref, gmix_ref, gpre_ref, wg_ref, wv_ref,
                    cw_ref, cb_ref, wd_ref, gpost_ref, out_ref,
                    uext, hmid, g_a, g_b, v_a, v_b, acc_s, *, tf, n_sub):
    tm = h_ref.shape[0]
    halo = BF16_ROWS
    rs = tm // n_sub
    first = pl.program_id(1) == 0
    n_chunks = D_FF // tf
    bufs = ((g_a, v_a), (g_b, v_b))

    @pl.when(first)
    def _():
        uext[0:halo, :] = jnp.zeros((halo, D_MODEL), BF16)

    @pl.when(jnp.logical_not(first))
    def _():
        uext[0:halo, :] = uext[tm:tm + halo, :]

    ys = [_dot(yr_ref[r * rs:(r + 1) * rs, :], wo_ref[0:LRU_WIDTH, :])
          + _dot(ya_ref[r * rs:(r + 1) * rs, :], wo_ref[LRU_WIDTH:, :]) for r in range(n_sub)]
    for r in range(n_sub):
        rows = slice(r * rs, (r + 1) * rs)
        hm = h_ref[rows, :] + _rms(ys[r], NORM_EPS) * gmix_ref[...]
        hmid[rows, :] = hm
        uext[halo + r * rs:halo + (r + 1) * rs, :] = (_rms(hm, NORM_EPS) * gpre_ref[...]).astype(BF16)

    def up_rows(c, lo, hi):
        g_s, v_s = bufs[c % 2]
        cols = slice(c * tf, (c + 1) * tf)
        glo = 0 if lo == 0 else halo + lo
        g_s[glo:halo + hi, :] = _dot(uext[glo:halo + hi, :], wg_ref[:, cols])
        v_s[lo:hi, :] = _dot(uext[halo + lo:halo + hi, :], wv_ref[:, cols])

    def gate_rows(c, lo, hi):
        g_s, v_s = bufs[c % 2]
        cols = slice(c * tf, (c + 1) * tf)
        gc = cb_ref[:, cols]
        for k in range(FFN_CONV):
            off = halo - (FFN_CONV - 1) + k
            gc = gc + g_s[off + lo:off + hi, :] * cw_ref[k:k + 1, cols]
        return (_gelu(gc) * v_s[lo:hi, :]).astype(BF16)

    for r in range(n_sub):
        up_rows(0, r * rs, (r + 1) * rs)

    for c in range(n_chunks - 1):
        cols = slice(c * tf, (c + 1) * tf)
        up_rows(c + 1, 0, tm)
        part = _dot(gate_rows(c, 0, tm), wd_ref[cols, :])
        if c == 0:
            acc_s[...] = part
        else:
            acc_s[...] += part

    c = n_chunks - 1
    cols = slice(c * tf, (c + 1) * tf)
    for r in range(n_sub):
        rows = slice(r * rs, (r + 1) * rs)
        f = acc_s[rows, :] + _dot(gate_rows(c, r * rs, (r + 1) * rs), wd_ref[cols, :])
        out_ref[rows, :] = hmid[rows, :] + _rms(f, NORM_EPS) * gpost_ref[...]


def _mix_ffn(h, yr, ya, wo, gmix, gpre, wg, wv, cw, cb, wd, gpost, tm, tf, n_sub):
    B, TP, _ = h.shape
    assert D_FF // tf >= 2 and tm % (n_sub * BF16_ROWS) == 0
    rows = lambda width: pl.BlockSpec((None, tm, width), lambda b, i: (b, i, 0))
    once = pl.Buffered(1)
    const = lambda shape: pl.BlockSpec(shape, lambda b, i: (0, 0), pipeline_mode=once)
    vec = _full((1, D_MODEL))
    return pl.pallas_call(
        functools.partial(_mix_ffn_kernel, tf=tf, n_sub=n_sub),
        grid=(B, TP // tm),
        in_specs=[rows(D_MODEL), rows(LRU_WIDTH), rows(ATT_WIDTH), const((D_MODEL, D_MODEL)),
                  vec, vec, const((D_MODEL, D_FF)), const((D_MODEL, D_FF)),
                  _full((FFN_CONV, D_FF)), _full((1, D_FF)), const((D_FF, D_MODEL)), vec],
        out_specs=rows(D_MODEL),
        out_shape=jax.ShapeDtypeStruct(h.shape, F32),
        scratch_shapes=[pltpu.VMEM((tm + BF16_ROWS, D_MODEL), BF16),
                        pltpu.VMEM((tm, D_MODEL), F32),
                        pltpu.VMEM((tm + BF16_ROWS, tf), F32),
                        pltpu.VMEM((tm + BF16_ROWS, tf), F32),
                        pltpu.VMEM((tm, tf), F32), pltpu.VMEM((tm, tf), F32),
                        pltpu.VMEM((tm, D_MODEL), F32)],
        compiler_params=_params("parallel", "arbitrary"),
        name="mix_ffn",
    )(h, yr, ya, wo, gmix, gpre, wg, wv, cw, cb, wd, gpost)


def _block_diag(w):
    per = GATE_TILE // (LRU_WIDTH // LRU_BLOCKS)
    bd = w.shape[-1]
    w = w.reshape(LRU_WIDTH // GATE_TILE, per, bd, bd)
    eye = jnp.eye(per, dtype=w.dtype)
    out = jnp.einsum('jpcd,pq->jpcqd', w, eye)
    return out.reshape(LRU_WIDTH // GATE_TILE, GATE_TILE, GATE_TILE).astype(BF16)


def _keep_rows(tt):
    p = jnp.arange(tt)
    orig = (p % SUBLANES) * (tt // SUBLANES) + p // SUBLANES
    keep = jnp.stack([(orig >= PAD).astype(F32), jnp.ones(tt, F32)])
    return jnp.broadcast_to(keep[:, :, None], (2, tt, LANES))


def _rope_tables(seq):
    inv = 1.0 / (ROPE_THETA ** (jnp.arange(0, HEAD_DIM, 2, dtype=F32) / HEAD_DIM))
    pos = jnp.concatenate([jnp.zeros(PAD, jnp.int32), jnp.arange(N_META + seq)]).astype(F32)
    ang = pos[:, None] * inv[None, :]
    cos, sin = jnp.cos(ang), jnp.sin(ang)
    reps = LANES // HEAD_DIM
    return (jnp.tile(jnp.concatenate([cos, cos], axis=1), (1, reps)),
            jnp.tile(jnp.concatenate([-sin, sin], axis=1), (1, reps)))


def kernel(x, meta_tokens, ln_mix_pre, ln_mix_post, ln_ffn_pre, ln_ffn_post, w_in, lru_conv_w, lru_conv_b, lru_wa, lru_ba, lru_wx, lru_bx, lru_a_param, lru_out_norm, lam_q1, lam_k1, lam_q2, lam_k2, diff_subln, w_out, w_up, ffn_conv_w, ffn_conv_b, w_down):
    B, S, _ = x.shape
    depth = w_in.shape[0]
    TP = FRONT + S
    assert TP % 384 == 0 and TP % 528 == 0 and S % 512 == 0
    meta = jnp.broadcast_to(meta_tokens[None].astype(x.dtype), (B, N_META, D_MODEL))
    h = jnp.concatenate([jnp.zeros((B, PAD, D_MODEL), x.dtype), meta, x], axis=1)
    cos, sin = _rope_tables(S)
    vec = lambda a: a.reshape(1, -1)

    for l in range(depth):
        lambda_init = 0.8 - 0.6 * math.exp(-0.3 * l)
        xr, gr, qt, k, v = _in_proj(h, vec(ln_mix_pre[l]), w_in[l].astype(BF16), cos, sin, tm=384)
        yr = _lru(xr, gr, _keep_rows(528), lru_conv_w[l], vec(lru_conv_b[l]),
                  _block_diag(lru_wa[l]), vec(lru_ba[l]), _block_diag(lru_wx[l]), vec(lru_bx[l]),
                  vec(lru_a_param[l]), vec(lru_out_norm[l]), tt=528)
        lam = jnp.stack([lam_q1[l], lam_k1[l], lam_q2[l], lam_k2[l]])
        ya = _attn(qt, k, v, lam, vec(diff_subln[l]), lambda_init, tq=512)
        wu = w_up[l].astype(BF16)
        h = _mix_ffn(h, yr, ya, w_out[l].astype(BF16), vec(ln_mix_post[l]), vec(ln_ffn_pre[l]),
                     wu[:, :D_FF], wu[:, D_FF:], ffn_conv_w[l], vec(ffn_conv_b[l]),
                     w_down[l].astype(BF16), vec(ln_ffn_post[l]), tm=528, tf=512, n_sub=3)
    return h[:, FRONT:]
```

```python
import functools
import math

import jax
import jax.numpy as jnp
from jax import lax
from jax.experimental import pallas as pl
from jax.experimental.pallas import tpu as pltpu

D_MODEL = 1024
N_META = 16
CHUNK = 64
LRU_WIDTH = 512
LRU_BLOCKS = 8
LRU_CONV = 4
LRU_C = 8.0
ATT_WIDTH = 512
HEADS = 4
HEAD_DIM = 64
V_DIM = 128
ROPE_THETA = 10000.0
D_FF = 3 * D_MODEL
FFN_CONV = 3
NORM_EPS = 1e-6
SUBLN_EPS = 1e-5
IN_WIDTH = 2 * LRU_WIDTH + 3 * ATT_WIDTH
NEG_INF = -1e30

LANES = 128
SUBLANES = 8
BF16_ROWS = 16
GATE_TILE = 256
FRONT = LANES
PAD = FRONT - N_META

VMEM_LIMIT = 56 * 1024 * 1024

F32 = jnp.float32
BF16 = jnp.bfloat16


def _dot(a, b):
    return jnp.dot(a, b, preferred_element_type=F32)


def _dot_t(a, b):
    return lax.dot_general(a, b, (((0,), (0,)), ((), ())), preferred_element_type=F32)


def _gelu(x):
    c = math.sqrt(2.0 / math.pi)
    return 0.5 * x * (1.0 + jnp.tanh(c * (x + 0.044715 * (x * x * x))))


def _sigmoid(x):
    return 1.0 / (1.0 + jnp.exp(-x))


def _rms(x, eps):
    return x * lax.rsqrt(jnp.mean(x * x, axis=-1, keepdims=True) + eps)


def _params(*sem):
    return pltpu.CompilerParams(dimension_semantics=sem, vmem_limit_bytes=VMEM_LIMIT)


def _full(shape):
    n = len(shape)
    return pl.BlockSpec(shape, lambda *_: (0,) * n)


def _in_lru_kernel(h_ref, g_ref, w_ref, cos_ref, sin_ref, keep_ref, cw_ref, cb_ref,
                   wa_ref, ba_ref, wx_ref, bx_ref, ap_ref, gn_ref,
                   q_ref, k_ref, v_ref, yr_ref,
                   u_s, xr_s, gr_s, a_s, b_s, h_s, tails, carry, *, nt, n_sub):
    t = pl.program_id(0)
    tm = h_ref.shape[0]
    n_slabs = xr_s.shape[0]
    seg = tm // SUBLANES
    rs = tm // n_sub
    sub = lax.broadcasted_iota(jnp.int32, (SUBLANES, LRU_WIDTH), 0)
    fresh = ((t + nt - 1) % nt) == 0

    @pl.when(t == 0)
    def _():
        xr_s[...] = jnp.zeros_like(xr_s)
        gr_s[...] = jnp.zeros_like(gr_s)
        tails[...] = jnp.zeros_like(tails)
        carry[...] = jnp.zeros_like(carry)

    def permuted(j):
        return jnp.concatenate([xr_s[c, pl.ds(j, SUBLANES, stride=seg), :]
                                for c in range(n_slabs)], axis=1)

    xp = [permuted(j) for j in range(seg)]
    taps = LRU_CONV - 1
    heads = []
    for k in range(taps):
        prev = jnp.where(fresh, 0.0, tails[k * SUBLANES:(k + 1) * SUBLANES, :])
        heads.append(jnp.where(sub == 0, pltpu.roll(prev, 1, 0),
                               pltpu.roll(xp[seg - taps + k], 1, 0)))
    for k in range(taps):
        tails[k * SUBLANES:(k + 1) * SUBLANES, :] = xp[seg - taps + k]
    ext = jnp.concatenate(heads + xp, axis=0)
    xc = cb_ref[...]
    for k in range(LRU_CONV):
        xc = xc + ext[k * SUBLANES:k * SUBLANES + tm, :] * cw_ref[k:k + 1, :]
    xcb = xc.astype(BF16)

    for r in range(n_sub):
        rows = slice(r * rs, (r + 1) * rs)
        u_s[rows, :] = (_rms(h_ref[rows, :], NORM_EPS) * g_ref[...]).astype(BF16)
    o = 2 * LRU_WIDTH
    zq = jnp.concatenate([_dot(u_s[r * rs:(r + 1) * rs, :], w_ref[:, o:o + ATT_WIDTH])
                          for r in range(n_sub)], axis=0)
    zk = _dot(u_s[...], w_ref[:, o + ATT_WIDTH:o + 2 * ATT_WIDTH])
    v_ref[...] = _dot(u_s[...], w_ref[:, o + 2 * ATT_WIDTH:o + 3 * ATT_WIDTH]).astype(BF16)
    zx = _dot(u_s[...], w_ref[:, 0:LRU_WIDTH])
    for c in range(n_slabs):
        xr_s[c] = zx[:, c * LANES:(c + 1) * LANES]

    def gate(wg_ref, b_ref):
        parts = [_dot(xcb[:, j * GATE_TILE:(j + 1) * GATE_TILE], wg_ref[j])
                 for j in range(LRU_WIDTH // GATE_TILE)]
        return _sigmoid(jnp.concatenate(parts, axis=1) + b_ref[...])

    r_gate = gate(wa_ref, ba_ref)
    i_gate = gate(wx_ref, bx_ref)

    ap = ap_ref[...]
    log_sig = jnp.minimum(ap, 0.0) - jnp.log1p(jnp.exp(-jnp.abs(ap)))
    a = jnp.exp(LRU_C * r_gate * log_sig)
    a_s[...] = a
    d = 1.0 - a * a
    b = jnp.where(d > 0.0, d * lax.rsqrt(d), 0.0) * (i_gate * xc)
    keep = keep_ref[...]
    b_s[...] = b * jnp.concatenate([keep] * n_slabs, axis=1)

    vreg = lambda ref, j: ref[j * SUBLANES:(j + 1) * SUBLANES, :]

    h_end = jnp.zeros((SUBLANES, LRU_WIDTH), F32)
    a_prod = jnp.ones((SUBLANES, LRU_WIDTH), F32)
    for j in range(seg):
        aj = vreg(a_s, j)
        h_end = aj * h_end + vreg(b_s, j)
        a_prod = aj * a_prod

    c_in = jnp.where(fresh, 0.0, carry[...])
    enter = c_in
    for _ in range(SUBLANES - 1):
        enter = jnp.where(sub == 0, c_in, pltpu.roll(h_end + a_prod * enter, 1, 0))
    last = h_end + a_prod * enter
    carry[...] = jnp.broadcast_to(last[SUBLANES - 1:SUBLANES, :], (SUBLANES, LRU_WIDTH))

    hv = enter
    for j in range(seg):
        hv = vreg(a_s, j) * hv + vreg(b_s, j)
        for c in range(n_slabs):
            h_s[c, pl.ds(j, SUBLANES, stride=seg), :] = hv[:, c * LANES:(c + 1) * LANES]

    hn = jnp.concatenate([h_s[c] for c in range(n_slabs)], axis=1)
    y = hn * _gelu(gr_s[...])
    yr_ref[...] = (_rms(y, NORM_EPS) * gn_ref[...]).astype(BF16)

    gr_s[...] = _dot(u_s[...], w_ref[:, LRU_WIDTH:2 * LRU_WIDTH])

    cos = cos_ref[...]
    sin = sin_ref[...]
    lane = lax.broadcasted_iota(jnp.int32, (tm, LANES), 1)
    low_half = (lane & (HEAD_DIM // 2)) == 0

    def rope(x):
        up = pltpu.roll(x, LANES - HEAD_DIM // 2, 1)
        down = pltpu.roll(x, HEAD_DIM // 2, 1)
        return x * cos + jnp.where(low_half, up, down) * sin

    q_scale = HEAD_DIM ** -0.5 * math.log2(math.e)
    for hd in range(HEADS):
        sl = slice(hd * LANES, (hd + 1) * LANES)
        q_ref[:, sl] = (rope(zq[:, sl]) * q_scale).astype(BF16)
        k_ref[:, sl] = rope(zk[:, sl]).astype(BF16)


def _in_lru(h, g, w, cos, sin, keep, cw, cb, wa, ba, wx, bx, ap, gn, tm, n_sub):
    B, TP, _ = h.shape
    nt = TP // tm
    last = B * nt - 1
    assert tm % (n_sub * BF16_ROWS) == 0
    proj = lambda t: jnp.minimum(t, last)
    lru = lambda t: jnp.maximum(t - 1, 0)
    rows = lambda width, tile: pl.BlockSpec(
        (None, tm, width), lambda t: (tile(t) // nt, tile(t) % nt, 0))
    out = lambda width: jax.ShapeDtypeStruct((B, TP, width), BF16)
    vec = _full((1, LRU_WIDTH))
    wspec = _full((LRU_WIDTH // GATE_TILE, GATE_TILE, GATE_TILE))
    n_slabs = LRU_WIDTH // LANES
    taps = LRU_CONV - 1
    return pl.pallas_call(
        functools.partial(_in_lru_kernel, nt=nt, n_sub=n_sub),
        grid=(B * nt + 1,),
        in_specs=[rows(D_MODEL, proj), _full((1, D_MODEL)),
                  pl.BlockSpec((D_MODEL, IN_WIDTH), lambda t: (0, 0), pipeline_mode=pl.Buffered(1)),
                  pl.BlockSpec((tm, LANES), lambda t: (proj(t) % nt, 0)),
                  pl.BlockSpec((tm, LANES), lambda t: (proj(t) % nt, 0)),
                  pl.BlockSpec((None, tm, LANES), lambda t: (jnp.minimum(lru(t) % nt, 1), 0, 0)),
                  _full((LRU_CONV, LRU_WIDTH)), vec, wspec, vec, wspec, vec, vec, vec],
        out_specs=[rows(ATT_WIDTH, proj), rows(ATT_WIDTH, proj), rows(ATT_WIDTH, proj),
                   rows(LRU_WIDTH, lru)],
        out_shape=[out(ATT_WIDTH), out(ATT_WIDTH), out(ATT_WIDTH), out(LRU_WIDTH)],
        scratch_shapes=[pltpu.VMEM((tm, D_MODEL), BF16),
                        pltpu.VMEM((n_slabs, tm, LANES), F32),
                        pltpu.VMEM((tm, LRU_WIDTH), F32),
                        pltpu.VMEM((tm, LRU_WIDTH), F32),
                        pltpu.VMEM((tm, LRU_WIDTH), F32),
                        pltpu.VMEM((n_slabs, tm, LANES), F32),
                        pltpu.VMEM((taps * SUBLANES, LRU_WIDTH), F32),
                        pltpu.VMEM((SUBLANES, LRU_WIDTH), F32)],
        compiler_params=_params("arbitrary"),
        name="in_lru",
    )(h, g, w, cos, sin, keep, cw, cb, wa, ba, wx, bx, ap, gn)


def _attn_kernel(q_ref, k_ref, v_ref, lam_ref, sub_ref, out_ref,
                 qc_s, s0, s1, x0, x1, m_s, l_s, acc_s, *, lambda_init, tq):
    n_tiles = (q_ref.shape[0] - FRONT) // tq
    lp = lam_ref[...]
    lam = (jnp.exp(jnp.sum(lp[0:1] * lp[1:2], axis=1, keepdims=True))
           - jnp.exp(jnp.sum(lp[2:3] * lp[3:4], axis=1, keepdims=True)) + lambda_init)
    gain = sub_ref[...] * (1.0 - lambda_init)
    k_meta = k_ref[PAD:FRONT, :]
    v_meta = v_ref[PAD:FRONT, :]

    def qcat(q):
        dim = lax.broadcasted_iota(jnp.int32, q.shape, 1)
        zero = jnp.zeros_like(q)
        return jnp.concatenate([jnp.where(dim < HEAD_DIM, q, zero),
                                jnp.where(dim >= HEAD_DIM, q, zero)], axis=0)

    def qk(kb, qc):
        return lax.dot_general(kb, qc, (((1,), (1,)), ((), ())), preferred_element_type=F32)

    def meta_state(qc):
        s = qk(k_meta, qc)
        m = jnp.max(s, axis=0, keepdims=True)
        p = jnp.exp2(s - m)
        return m, jnp.sum(p, axis=0, keepdims=True), _dot_t(v_meta, p.astype(BF16))

    def finish(acc, l, rows):
        n = acc.shape[1] // 2
        o = acc * (1.0 / l)
        o = o[:, :n] - lam * o[:, n:]
        o = o * lax.rsqrt(jnp.mean(o * o, axis=0, keepdims=True) + SUBLN_EPS)
        out_ref[rows, :] = (o.T * gain).astype(BF16)

    def scores(s_ref, mx_ref, j):
        start = pl.multiple_of(FRONT + j * tq, LANES)
        kb = jnp.concatenate([k_ref[pl.ds(start, tq), :], k_meta], axis=0)
        s = qk(kb, qc_s[...])
        s_ref[...] = s
        mx_ref[...] = jnp.max(s[0:tq], axis=0, keepdims=True)

    def absorb(s_ref, mx_ref, j, diagonal):
        vb = v_ref[pl.ds(pl.multiple_of(FRONT + j * tq, LANES), tq), :]
        if diagonal:
            s = s_ref[...]
            row = lax.broadcasted_iota(jnp.int32, (tq + N_META, 1), 0)
            kc = jnp.where(row < tq, row // CHUNK, 0)
            qc_ = (lax.broadcasted_iota(jnp.int32, (1, 2 * tq), 1) % tq) // CHUNK
            s = jnp.where(kc <= qc_, s, NEG_INF)
            mx = jnp.max(s, axis=0, keepdims=True)
            vb = jnp.concatenate([vb, v_meta], axis=0)
        else:
            s = s_ref[0:tq, :]
            mx = mx_ref[...]
        m_old = m_s[...]
        m_new = jnp.maximum(m_old, mx)
        alpha = jnp.exp2(m_old - m_new)
        p = jnp.exp2(s - m_new)
        l_s[...] = alpha * l_s[...] + jnp.sum(p, axis=0, keepdims=True)
        m_s[...] = m_new
        acc_s[...] = alpha * acc_s[...] + _dot_t(vb, p.astype(BF16))

    m, l, acc = meta_state(qcat(q_ref[0:FRONT, :]))
    finish(acc, l, slice(0, FRONT))
    out_ref[0:PAD, :] = jnp.zeros((PAD, V_DIM), BF16)

    def load_queries(i):
        qc_s[...] = qcat(q_ref[pl.ds(pl.multiple_of(FRONT + i * tq, LANES), tq), :])

    load_queries(0)
    scores(s0, x0, 0)

    def tile(i, c):
        m_s[...] = jnp.full_like(m_s, NEG_INF)
        l_s[...] = jnp.zeros_like(l_s)
        acc_s[...] = jnp.zeros_like(acc_s)

        def pair(jj, c2):
            j = 2 * jj
            scores(s1, x1, j + 1)
            absorb(s0, x0, j, False)
            scores(s0, x0, j + 2)
            absorb(s1, x1, j + 1, False)
            return c2

        lax.fori_loop(0, i // 2, pair, 0)

        @pl.when(i % 2 == 1)
        def _():
            scores(s1, x1, i)
            absorb(s0, x0, i - 1, False)
            absorb(s1, x1, i, True)

        @pl.when(i % 2 == 0)
        def _():
            absorb(s0, x0, i, True)

        acc, l = acc_s[...], l_s[...]
        load_queries(jnp.minimum(i + 1, n_tiles - 1))
        scores(s0, x0, 0)
        finish(acc, l, pl.ds(pl.multiple_of(FRONT + i * tq, LANES), tq))
        return c

    lax.fori_loop(0, n_tiles, tile, 0)


def _attn(q, k, v, lam, sub, lambda_init, tq):
    B, TP, _ = k.shape
    kern = functools.partial(_attn_kernel, lambda_init=lambda_init, tq=tq)
    return pl.pallas_call(
        kern,
        grid=(B, HEADS),
        in_specs=[pl.BlockSpec((None, TP, V_DIM), lambda b, h: (b, 0, h)),
                  pl.BlockSpec((None, TP, V_DIM), lambda b, h: (b, 0, h)),
                  pl.BlockSpec((None, TP, V_DIM), lambda b, h: (b, 0, h)),
                  _full((4, HEAD_DIM)), _full((1, V_DIM))],
        out_specs=pl.BlockSpec((None, TP, V_DIM), lambda b, h: (b, 0, h)),
        out_shape=jax.ShapeDtypeStruct((B, TP, ATT_WIDTH), BF16),
        scratch_shapes=[pltpu.VMEM((2 * tq, V_DIM), BF16),
                        pltpu.VMEM((tq + N_META, 2 * tq), F32),
                        pltpu.VMEM((tq + N_META, 2 * tq), F32),
                        pltpu.VMEM((1, 2 * tq), F32), pltpu.VMEM((1, 2 * tq), F32),
                        pltpu.VMEM((1, 2 * tq), F32), pltpu.VMEM((1, 2 * tq), F32),
                        pltpu.VMEM((V_DIM, 2 * tq), F32)],
        compiler_params=_params("parallel", "parallel"),
        name="diff_attn",
    )(q, k, v, lam, sub)


def _mix_ffn_kernel(h_ref, yr_ref, ya_ref, wo_ref, gmix_ref, gpre_ref, wg_ref, wv_ref,
                    cw_ref, cb_ref, wd_ref, gpost_ref, out_ref,
                    uext, hmid, g_a, g_b, v_a, v_b, acc_s, *, tf, n_sub):
    tm = h_ref.shape[0]
    halo = BF16_ROWS
    rs = tm // n_sub
    first = pl.program_id(1) == 0
    n_chunks = D_FF // tf
    bufs = ((g_a, v_a), (g_b, v_b))

    @pl.when(first)
    def _():
        uext[0:halo, :] = jnp.zeros((halo, D_MODEL), BF16)

    @pl.when(jnp.logical_not(first))
    def _():
        uext[0:halo, :] = uext[tm:tm + halo, :]

    ys = [_dot(yr_ref[r * rs:(r + 1) * rs, :], wo_ref[0:LRU_WIDTH, :])
          + _dot(ya_ref[r * rs:(r + 1) * rs, :], wo_ref[LRU_WIDTH:, :]) for r in range(n_sub)]
    for r in range(n_sub):
        rows = slice(r * rs, (r + 1) * rs)
        hm = h_ref[rows, :] + _rms(ys[r], NORM_EPS) * gmix_ref[...]
        hmid[rows, :] = hm
        uext[halo + r * rs:halo + (r + 1) * rs, :] = (_rms(hm, NORM_EPS) * gpre_ref[...]).astype(BF16)

    def up_rows(c, lo, hi):
        g_s, v_s = bufs[c % 2]
        cols = slice(c * tf, (c + 1) * tf)
        glo = 0 if lo == 0 else halo + lo
        g_s[glo:halo + hi, :] = _dot(uext[glo:halo + hi, :], wg_ref[:, cols])
        v_s[lo:hi, :] = _dot(uext[halo + lo:halo + hi, :], wv_ref[:, cols])

    def gate_rows(c, lo, hi):
        g_s, v_s = bufs[c % 2]
        cols = slice(c * tf, (c + 1) * tf)
        gc = cb_ref[:, cols]
        for k in range(FFN_CONV):
            off = halo - (FFN_CONV - 1) + k
            gc = gc + g_s[off + lo:off + hi, :] * cw_ref[k:k + 1, cols]
        return (_gelu(gc) * v_s[lo:hi, :]).astype(BF16)

    for r in range(n_sub):
        up_rows(0, r * rs, (r + 1) * rs)

    for c in range(n_chunks - 1):
        cols = slice(c * tf, (c + 1) * tf)
        up_rows(c + 1, 0, tm)
        part = _dot(gate_rows(c, 0, tm), wd_ref[cols, :])
        if c == 0:
            acc_s[...] = part
        else:
            acc_s[...] += part

    c = n_chunks - 1
    cols = slice(c * tf, (c + 1) * tf)
    for r in range(n_sub):
        rows = slice(r * rs, (r + 1) * rs)
        f = acc_s[rows, :] + _dot(gate_rows(c, r * rs, (r + 1) * rs), wd_ref[cols, :])
        out_ref[rows, :] = hmid[rows, :] + _rms(f, NORM_EPS) * gpost_ref[...]


def _mix_ffn(h, yr, ya, wo, gmix, gpre, wg, wv, cw, cb, wd, gpost, tm, tf, n_sub):
    B, TP, _ = h.shape
    assert D_FF // tf >= 2 and tm % (n_sub * BF16_ROWS) == 0
    rows = lambda width: pl.BlockSpec((None, tm, width), lambda b, i: (b, i, 0))
    once = pl.Buffered(1)
    const = lambda shape: pl.BlockSpec(shape, lambda b, i: (0, 0), pipeline_mode=once)
    vec = _full((1, D_MODEL))
    return pl.pallas_call(
        functools.partial(_mix_ffn_kernel, tf=tf, n_sub=n_sub),
        grid=(B, TP // tm),
        in_specs=[rows(D_MODEL), rows(LRU_WIDTH), rows(ATT_WIDTH), const((D_MODEL, D_MODEL)),
                  vec, vec, const((D_MODEL, D_FF)), const((D_MODEL, D_FF)),
                  _full((FFN_CONV, D_FF)), _full((1, D_FF)), const((D_FF, D_MODEL)), vec],
        out_specs=rows(D_MODEL),
        out_shape=jax.ShapeDtypeStruct(h.shape, F32),
        scratch_shapes=[pltpu.VMEM((tm + BF16_ROWS, D_MODEL), BF16),
                        pltpu.VMEM((tm, D_MODEL), F32),
                        pltpu.VMEM((tm + BF16_ROWS, tf), F32),
                        pltpu.VMEM((tm + BF16_ROWS, tf), F32),
                        pltpu.VMEM((tm, tf), F32), pltpu.VMEM((tm, tf), F32),
                        pltpu.VMEM((tm, D_MODEL), F32)],
        compiler_params=_params("parallel", "arbitrary"),
        name="mix_ffn",
    )(h, yr, ya, wo, gmix, gpre, wg, wv, cw, cb, wd, gpost)


def _block_diag(w):
    per = GATE_TILE // (LRU_WIDTH // LRU_BLOCKS)
    bd = w.shape[-1]
    w = w.reshape(LRU_WIDTH // GATE_TILE, per, bd, bd)
    eye = jnp.eye(per, dtype=w.dtype)
    out = jnp.einsum('jpcd,pq->jpcqd', w, eye)
    return out.reshape(LRU_WIDTH // GATE_TILE, GATE_TILE, GATE_TILE).astype(BF16)


def _keep_rows(tt):
    p = jnp.arange(tt)
    orig = (p % SUBLANES) * (tt // SUBLANES) + p // SUBLANES
    keep = jnp.stack([(orig >= PAD).astype(F32), jnp.ones(tt, F32)])
    return jnp.broadcast_to(keep[:, :, None], (2, tt, LANES))


def _rope_tables(seq):
    inv = 1.0 / (ROPE_THETA ** (jnp.arange(0, HEAD_DIM, 2, dtype=F32) / HEAD_DIM))
    pos = jnp.concatenate([jnp.zeros(PAD, jnp.int32), jnp.arange(N_META + seq)]).astype(F32)
    ang = pos[:, None] * inv[None, :]
    cos, sin = jnp.cos(ang), jnp.sin(ang)
    reps = LANES // HEAD_DIM
    return (jnp.tile(jnp.concatenate([cos, cos], axis=1), (1, reps)),
            jnp.tile(jnp.concatenate([-sin, sin], axis=1), (1, reps)))


def kernel(x, meta_tokens, ln_mix_pre, ln_mix_post, ln_ffn_pre, ln_ffn_post, w_in, lru_conv_w, lru_conv_b, lru_wa, lru_ba, lru_wx, lru_bx, lru_a_param, lru_out_norm, lam_q1, lam_k1, lam_q2, lam_k2, diff_subln, w_out, w_up, ffn_conv_w, ffn_conv_b, w_down):
    B, S, _ = x.shape
    depth = w_in.shape[0]
    TP = FRONT + S
    assert TP % 528 == 0 and S % 512 == 0
    meta = jnp.broadcast_to(meta_tokens[None].astype(x.dtype), (B, N_META, D_MODEL))
    h = jnp.concatenate([jnp.zeros((B, PAD, D_MODEL), x.dtype), meta, x], axis=1)
    cos, sin = _rope_tables(S)
    vec = lambda a: a.reshape(1, -1)

    for l in range(depth):
        lambda_init = 0.8 - 0.6 * math.exp(-0.3 * l)
        q, k, v, yr = _in_lru(h, vec(ln_mix_pre[l]), w_in[l].astype(BF16), cos, sin, _keep_rows(528),
                              lru_conv_w[l], vec(lru_conv_b[l]),
                              _block_diag(lru_wa[l]), vec(lru_ba[l]), _block_diag(lru_wx[l]),
                              vec(lru_bx[l]), vec(lru_a_param[l]), vec(lru_out_norm[l]),
                              tm=528, n_sub=3)
        lam = jnp.stack([lam_q1[l], lam_k1[l], lam_q2[l], lam_k2[l]])
        ya = _attn(q, k, v, lam, vec(diff_subln[l]), lambda_init, tq=512)
        wu = w_up[l].astype(BF16)
        h = _mix_ffn(h, yr, ya, w_out[l].astype(BF16), vec(ln_mix_post[l]), vec(ln_ffn_pre[l]),
                     wu[:, :D_FF], wu[:, D_FF:], ffn_conv_w[l], vec(ffn_conv_b[l]),
                     w_down[l].astype(BF16), vec(ln_ffn_post[l]), tm=528, tf=512, n_sub=3)
    return h[:, FRONT:]
```

```python
import functools
import math

import jax
import jax.numpy as jnp
from jax import lax
from jax.experimental import pallas as pl
from jax.experimental.pallas import tpu as pltpu

D_MODEL = 1024
N_META = 16
CHUNK = 64
LRU_WIDTH = 512
LRU_BLOCKS = 8
LRU_CONV = 4
LRU_C = 8.0
ATT_WIDTH = 512
HEADS = 4
HEAD_DIM = 64
V_DIM = 128
ROPE_THETA = 10000.0
D_FF = 3 * D_MODEL
FFN_CONV = 3
NORM_EPS = 1e-6
SUBLN_EPS = 1e-5
IN_WIDTH = 2 * LRU_WIDTH + 3 * ATT_WIDTH
NEG_INF = -1e30

LANES = 128
SUBLANES = 8
BF16_ROWS = 16
GATE_TILE = 256
FRONT = LANES
PAD = FRONT - N_META

VMEM_LIMIT = 56 * 1024 * 1024

F32 = jnp.float32
BF16 = jnp.bfloat16


def _dot(a, b):
    return jnp.dot(a, b, preferred_element_type=F32)


def _dot_t(a, b):
    return lax.dot_general(a, b, (((0,), (0,)), ((), ())), preferred_element_type=F32)


def _gelu(x):
    c = math.sqrt(2.0 / math.pi)
    return 0.5 * x * (1.0 + jnp.tanh(c * (x + 0.044715 * (x * x * x))))


def _sigmoid(x):
    return 1.0 / (1.0 + jnp.exp(-x))


def _rms(x, eps):
    return x * lax.rsqrt(jnp.mean(x * x, axis=-1, keepdims=True) + eps)


def _residual_rows(h_ref, front_ref, first, lo, hi, embed):
    direct = h_ref[lo:hi, :]
    if not embed:
        return direct
    if hi <= FRONT:
        shifted = front_ref[lo:hi, :]
    elif lo >= FRONT:
        shifted = h_ref[lo - FRONT:hi - FRONT, :]
    else:
        shifted = jnp.concatenate([front_ref[lo:FRONT, :], h_ref[0:hi - FRONT, :]], axis=0)
    return jnp.where(first, shifted, direct)


def _residual_spec(tm, embed, tile):
    if not embed:
        return pl.BlockSpec((None, tm, D_MODEL), lambda *ids: (*tile(*ids), 0))

    def window(*ids):
        b, i = tile(*ids)
        return b, pl.multiple_of(jnp.maximum(i * tm - FRONT, 0), SUBLANES), 0

    return pl.BlockSpec((None, pl.Element(tm), pl.Element(D_MODEL)), window)


def _params(*sem):
    return pltpu.CompilerParams(dimension_semantics=sem, vmem_limit_bytes=VMEM_LIMIT)


def _full(shape):
    n = len(shape)
    return pl.BlockSpec(shape, lambda *_: (0,) * n)


def _in_lru_kernel(h_ref, front_ref, g_ref, w_ref, cos_ref, sin_ref, keep_ref, cw_ref, cb_ref,
                   wa_ref, ba_ref, wx_ref, bx_ref, ap_ref, gn_ref,
                   q_ref, k_ref, v_ref, yr_ref,
                   u_s, xr_s, gr_s, a_s, b_s, h_s, tails, carry, *, nt, n_sub, embed):
    t = pl.program_id(0)
    tm = h_ref.shape[0]
    n_slabs = xr_s.shape[0]
    seg = tm // SUBLANES
    rs = tm // n_sub
    sub = lax.broadcasted_iota(jnp.int32, (SUBLANES, LRU_WIDTH), 0)
    fresh = ((t + nt - 1) % nt) == 0
    lead = (jnp.minimum(t, pl.num_programs(0) - 2) % nt) == 0

    @pl.when(t == 0)
    def _():
        xr_s[...] = jnp.zeros_like(xr_s)
        gr_s[...] = jnp.zeros_like(gr_s)
        tails[...] = jnp.zeros_like(tails)
        carry[...] = jnp.zeros_like(carry)

    def permuted(j):
        return jnp.concatenate([xr_s[c, pl.ds(j, SUBLANES, stride=seg), :]
                                for c in range(n_slabs)], axis=1)

    xp = [permuted(j) for j in range(seg)]
    taps = LRU_CONV - 1
    heads = []
    for k in range(taps):
        prev = jnp.where(fresh, 0.0, tails[k * SUBLANES:(k + 1) * SUBLANES, :])
        heads.append(jnp.where(sub == 0, pltpu.roll(prev, 1, 0),
                               pltpu.roll(xp[seg - taps + k], 1, 0)))
    for k in range(taps):
        tails[k * SUBLANES:(k + 1) * SUBLANES, :] = xp[seg - taps + k]
    ext = jnp.concatenate(heads + xp, axis=0)
    xc = cb_ref[...]
    for k in range(LRU_CONV):
        xc = xc + ext[k * SUBLANES:k * SUBLANES + tm, :] * cw_ref[k:k + 1, :]
    xcb = xc.astype(BF16)

    for r in range(n_sub):
        rows = slice(r * rs, (r + 1) * rs)
        hr = _residual_rows(h_ref, front_ref, lead, r * rs, (r + 1) * rs, embed)
        u_s[rows, :] = (_rms(hr, NORM_EPS) * g_ref[...]).astype(BF16)
    o = 2 * LRU_WIDTH
    zq = jnp.concatenate([_dot(u_s[r * rs:(r + 1) * rs, :], w_ref[:, o:o + ATT_WIDTH])
                          for r in range(n_sub)], axis=0)
    zk = _dot(u_s[...], w_ref[:, o + ATT_WIDTH:o + 2 * ATT_WIDTH])
    v_ref[...] = _dot(u_s[...], w_ref[:, o + 2 * ATT_WIDTH:o + 3 * ATT_WIDTH]).astype(BF16)
    zx = _dot(u_s[...], w_ref[:, 0:LRU_WIDTH])
    for c in range(n_slabs):
        xr_s[c] = zx[:, c * LANES:(c + 1) * LANES]

    def gate(wg_ref, b_ref):
        parts = [_dot(xcb[:, j * GATE_TILE:(j + 1) * GATE_TILE], wg_ref[j])
                 for j in range(LRU_WIDTH // GATE_TILE)]
        return _sigmoid(jnp.concatenate(parts, axis=1) + b_ref[...])

    r_gate = gate(wa_ref, ba_ref)
    i_gate = gate(wx_ref, bx_ref)

    ap = ap_ref[...]
    log_sig = jnp.minimum(ap, 0.0) - jnp.log1p(jnp.exp(-jnp.abs(ap)))
    a = jnp.exp2(r_gate * (LRU_C * math.log2(math.e) * log_sig))
    a_s[...] = a
    d = 1.0 - a * a
    b = jnp.where(d > 0.0, d * lax.rsqrt(d), 0.0) * (i_gate * xc)
    keep = keep_ref[...]
    b_s[...] = b * jnp.concatenate([keep] * n_slabs, axis=1)

    vreg = lambda ref, j: ref[j * SUBLANES:(j + 1) * SUBLANES, :]

    h_end = jnp.zeros((SUBLANES, LRU_WIDTH), F32)
    a_prod = jnp.ones((SUBLANES, LRU_WIDTH), F32)
    for j in range(seg):
        aj = vreg(a_s, j)
        h_end = aj * h_end + vreg(b_s, j)
        a_prod = aj * a_prod

    c_in = jnp.where(fresh, 0.0, carry[...])
    enter = c_in
    for _ in range(SUBLANES - 1):
        enter = jnp.where(sub == 0, c_in, pltpu.roll(h_end + a_prod * enter, 1, 0))
    last = h_end + a_prod * enter
    carry[...] = jnp.broadcast_to(last[SUBLANES - 1:SUBLANES, :], (SUBLANES, LRU_WIDTH))

    hv = enter
    for j in range(seg):
        hv = vreg(a_s, j) * hv + vreg(b_s, j)
        for c in range(n_slabs):
            h_s[c, pl.ds(j, SUBLANES, stride=seg), :] = hv[:, c * LANES:(c + 1) * LANES]

    hn = jnp.concatenate([h_s[c] for c in range(n_slabs)], axis=1)
    y = hn * _gelu(gr_s[...])
    yr_ref[...] = (_rms(y, NORM_EPS) * gn_ref[...]).astype(BF16)

    gr_s[...] = _dot(u_s[...], w_ref[:, LRU_WIDTH:2 * LRU_WIDTH])

    cos = cos_ref[...]
    sin = sin_ref[...]
    lane = lax.broadcasted_iota(jnp.int32, (tm, LANES), 1)
    low_half = (lane & (HEAD_DIM // 2)) == 0

    def rope(x):
        up = pltpu.roll(x, LANES - HEAD_DIM // 2, 1)
        down = pltpu.roll(x, HEAD_DIM // 2, 1)
        return x * cos + jnp.where(low_half, up, down) * sin

    q_scale = HEAD_DIM ** -0.5 * math.log2(math.e)
    for hd in range(HEADS):
        sl = slice(hd * LANES, (hd + 1) * LANES)
        q_ref[:, sl] = (rope(zq[:, sl]) * q_scale).astype(BF16)
        k_ref[:, sl] = rope(zk[:, sl]).astype(BF16)


def _in_lru(h, front, g, w, cos, sin, keep, cw, cb, wa, ba, wx, bx, ap, gn, tm, n_sub, embed):
    B = h.shape[0]
    TP = h.shape[1] + (FRONT if embed else 0)
    nt = TP // tm
    last = B * nt - 1
    assert tm % (n_sub * BF16_ROWS) == 0
    proj = lambda t: jnp.minimum(t, last)
    lru = lambda t: jnp.maximum(t - 1, 0)
    rows = lambda width, tile: pl.BlockSpec(
        (None, tm, width), lambda t: (tile(t) // nt, tile(t) % nt, 0))
    out = lambda width: jax.ShapeDtypeStruct((B, TP, width), BF16)
    vec = _full((1, LRU_WIDTH))
    wspec = _full((LRU_WIDTH // GATE_TILE, GATE_TILE, GATE_TILE))
    n_slabs = LRU_WIDTH // LANES
    taps = LRU_CONV - 1
    return pl.pallas_call(
        functools.partial(_in_lru_kernel, nt=nt, n_sub=n_sub, embed=embed),
        grid=(B * nt + 1,),
        in_specs=[_residual_spec(tm, embed, lambda t: (proj(t) // nt, proj(t) % nt)),
                  _full((FRONT, D_MODEL)), _full((1, D_MODEL)),
                  pl.BlockSpec((D_MODEL, IN_WIDTH), lambda t: (0, 0), pipeline_mode=pl.Buffered(1)),
                  pl.BlockSpec((tm, LANES), lambda t: (proj(t) % nt, 0)),
                  pl.BlockSpec((tm, LANES), lambda t: (proj(t) % nt, 0)),
                  pl.BlockSpec((None, tm, LANES), lambda t: (jnp.minimum(lru(t) % nt, 1), 0, 0)),
                  _full((LRU_CONV, LRU_WIDTH)), vec, wspec, vec, wspec, vec, vec, vec],
        out_specs=[rows(ATT_WIDTH, proj), rows(ATT_WIDTH, proj), rows(ATT_WIDTH, proj),
                   rows(LRU_WIDTH, lru)],
        out_shape=[out(ATT_WIDTH), out(ATT_WIDTH), out(ATT_WIDTH), out(LRU_WIDTH)],
        scratch_shapes=[pltpu.VMEM((tm, D_MODEL), BF16),
                        pltpu.VMEM((n_slabs, tm, LANES), F32),
                        pltpu.VMEM((tm, LRU_WIDTH), F32),
                        pltpu.VMEM((tm, LRU_WIDTH), F32),
                        pltpu.VMEM((tm, LRU_WIDTH), F32),
                        pltpu.VMEM((n_slabs, tm, LANES), F32),
                        pltpu.VMEM((taps * SUBLANES, LRU_WIDTH), F32),
                        pltpu.VMEM((SUBLANES, LRU_WIDTH), F32)],
        compiler_params=_params("arbitrary"),
        name="in_lru",
    )(h, front, g, w, cos, sin, keep, cw, cb, wa, ba, wx, bx, ap, gn)


def _attn_kernel(q_ref, k_ref, v_ref, lam_ref, sub_ref, out_ref,
                 qc_s, s0, s1, x0, x1, m_s, l_s, acc_s, *, lambda_init, tq):
    n_tiles = (q_ref.shape[0] - FRONT) // tq
    lp = lam_ref[...]
    lam = (jnp.exp(jnp.sum(lp[0:1] * lp[1:2], axis=1, keepdims=True))
           - jnp.exp(jnp.sum(lp[2:3] * lp[3:4], axis=1, keepdims=True)) + lambda_init)
    gain = sub_ref[...] * (1.0 - lambda_init)
    k_meta = k_ref[PAD:FRONT, :]
    v_meta = v_ref[PAD:FRONT, :]

    def qcat(q):
        qt = q.T
        dim = lax.broadcasted_iota(jnp.int32, qt.shape, 0)
        zero = jnp.zeros_like(qt)
        return jnp.concatenate([jnp.where(dim < HEAD_DIM, qt, zero),
                                jnp.where(dim >= HEAD_DIM, qt, zero)], axis=1)

    def meta_state(qc):
        s = _dot(k_meta, qc)
        m = jnp.max(s, axis=0, keepdims=True)
        p = jnp.exp2(s - m)
        return m, jnp.sum(p, axis=0, keepdims=True), _dot_t(v_meta, p.astype(BF16))

    def finish(acc, l, rows):
        n = acc.shape[1] // 2
        o = acc * (1.0 / l)
        o = o[:, :n] - lam * o[:, n:]
        o = o * lax.rsqrt(jnp.mean(o * o, axis=0, keepdims=True) + SUBLN_EPS)
        out_ref[rows, :] = (o.T * gain).astype(BF16)

    def scores(s_ref, mx_ref, j):
        start = pl.multiple_of(FRONT + j * tq, LANES)
        kb = jnp.concatenate([k_ref[pl.ds(start, tq), :], k_meta], axis=0)
        s = _dot(kb, qc_s[...])
        s_ref[...] = s
        mx_ref[...] = jnp.max(s[0:tq], axis=0, keepdims=True)

    def absorb(s_ref, mx_ref, j, diagonal):
        vb = v_ref[pl.ds(pl.multiple_of(FRONT + j * tq, LANES), tq), :]
        if diagonal:
            s = s_ref[...]
            row = lax.broadcasted_iota(jnp.int32, (tq + N_META, 1), 0)
            kc = jnp.where(row < tq, row // CHUNK, 0)
            qc_ = (lax.broadcasted_iota(jnp.int32, (1, 2 * tq), 1) % tq) // CHUNK
            s = jnp.where(kc <= qc_, s, NEG_INF)
            mx = jnp.max(s, axis=0, keepdims=True)
            vb = jnp.concatenate([vb, v_meta], axis=0)
        else:
            s = s_ref[0:tq, :]
            mx = mx_ref[...]
        m_old = m_s[...]
        m_new = jnp.maximum(m_old, mx)
        alpha = jnp.exp2(m_old - m_new)
        p = jnp.exp2(s - m_new)
        l_s[...] = alpha * l_s[...] + jnp.sum(p, axis=0, keepdims=True)
        m_s[...] = m_new
        acc_s[...] = alpha * acc_s[...] + _dot_t(vb, p.astype(BF16))

    m, l, acc = meta_state(qcat(q_ref[0:FRONT, :]))
    finish(acc, l, slice(0, FRONT))
    out_ref[0:PAD, :] = jnp.zeros((PAD, V_DIM), BF16)

    def load_queries(i):
        qc_s[...] = qcat(q_ref[pl.ds(pl.multiple_of(FRONT + i * tq, LANES), tq), :])

    load_queries(0)
    scores(s0, x0, 0)

    def tile(i, c):
        m_s[...] = jnp.full_like(m_s, NEG_INF)
        l_s[...] = jnp.zeros_like(l_s)
        acc_s[...] = jnp.zeros_like(acc_s)

        def pair(jj, c2):
            j = 2 * jj
            scores(s1, x1, j + 1)
            absorb(s0, x0, j, False)
            scores(s0, x0, j + 2)
            absorb(s1, x1, j + 1, False)
            return c2

        lax.fori_loop(0, i // 2, pair, 0)

        @pl.when(i % 2 == 1)
        def _():
            scores(s1, x1, i)
            absorb(s0, x0, i - 1, False)
            absorb(s1, x1, i, True)

        @pl.when(i % 2 == 0)
        def _():
            absorb(s0, x0, i, True)

        acc, l = acc_s[...], l_s[...]
        load_queries(jnp.minimum(i + 1, n_tiles - 1))
        scores(s0, x0, 0)
        finish(acc, l, pl.ds(pl.multiple_of(FRONT + i * tq, LANES), tq))
        return c

    lax.fori_loop(0, n_tiles, tile, 0)


def _attn(q, k, v, lam, sub, lambda_init, tq):
    B, TP, _ = k.shape
    kern = functools.partial(_attn_kernel, lambda_init=lambda_init, tq=tq)
    return pl.pallas_call(
        kern,
        grid=(B, HEADS),
        in_specs=[pl.BlockSpec((None, TP, V_DIM), lambda b, h: (b, 0, h)),
                  pl.BlockSpec((None, TP, V_DIM), lambda b, h: (b, 0, h)),
                  pl.BlockSpec((None, TP, V_DIM), lambda b, h: (b, 0, h)),
                  _full((4, HEAD_DIM)), _full((1, V_DIM))],
        out_specs=pl.BlockSpec((None, TP, V_DIM), lambda b, h: (b, 0, h)),
        out_shape=jax.ShapeDtypeStruct((B, TP, ATT_WIDTH), BF16),
        scratch_shapes=[pltpu.VMEM((V_DIM, 2 * tq), BF16),
                        pltpu.VMEM((tq + N_META, 2 * tq), F32),
                        pltpu.VMEM((tq + N_META, 2 * tq), F32),
                        pltpu.VMEM((1, 2 * tq), F32), pltpu.VMEM((1, 2 * tq), F32),
                        pltpu.VMEM((1, 2 * tq), F32), pltpu.VMEM((1, 2 * tq), F32),
                        pltpu.VMEM((V_DIM, 2 * tq), F32)],
        compiler_params=_params("parallel", "parallel"),
        name="diff_attn",
    )(q, k, v, lam, sub)


def _mix_ffn_kernel(h_ref, front_ref, yr_ref, ya_ref, wo_ref, gmix_ref, gpre_ref, wg_ref, wv_ref,
                    cw_ref, cb_ref, wd_ref, gpost_ref, out_ref,
                    uext, hmid, g_a, g_b, v_a, v_b, acc_s, *, tf, n_sub, embed):
    tm = h_ref.shape[0]
    halo = BF16_ROWS
    rs = tm // n_sub
    first = pl.program_id(1) == 0
    n_chunks = D_FF // tf
    bufs = ((g_a, v_a), (g_b, v_b))

    @pl.when(first)
    def _():
        uext[0:halo, :] = jnp.zeros((halo, D_MODEL), BF16)

    @pl.when(jnp.logical_not(first))
    def _():
        uext[0:halo, :] = uext[tm:tm + halo, :]

    ys = [_dot(yr_ref[r * rs:(r + 1) * rs, :], wo_ref[0:LRU_WIDTH, :])
          + _dot(ya_ref[r * rs:(r + 1) * rs, :], wo_ref[LRU_WIDTH:, :]) for r in range(n_sub)]
    for r in range(n_sub):
        rows = slice(r * rs, (r + 1) * rs)
        hm = (_residual_rows(h_ref, front_ref, first, r * rs, (r + 1) * rs, embed)
              + _rms(ys[r], NORM_EPS) * gmix_ref[...])
        hmid[rows, :] = hm
        uext[halo + r * rs:halo + (r + 1) * rs, :] = (_rms(hm, NORM_EPS) * gpre_ref[...]).astype(BF16)

    def up_rows(c, lo, hi):
        g_s, v_s = bufs[c % 2]
        cols = slice(c * tf, (c + 1) * tf)
        glo = 0 if lo == 0 else halo + lo
        g_s[glo:halo + hi, :] = _dot(uext[glo:halo + hi, :], wg_ref[:, cols])
        v_s[lo:hi, :] = _dot(uext[halo + lo:halo + hi, :], wv_ref[:, cols])

    def gate_rows(c, lo, hi):
        g_s, v_s = bufs[c % 2]
        cols = slice(c * tf, (c + 1) * tf)
        gc = cb_ref[:, cols]
        for k in range(FFN_CONV):
            off = halo - (FFN_CONV - 1) + k
            gc = gc + g_s[off + lo:off + hi, :] * cw_ref[k:k + 1, cols]
        return (_gelu(gc) * v_s[lo:hi, :]).astype(BF16)

    for r in range(n_sub):
        up_rows(0, r * rs, (r + 1) * rs)

    for c in range(n_chunks - 1):
        cols = slice(c * tf, (c + 1) * tf)
        up_rows(c + 1, 0, tm)
        part = _dot(gate_rows(c, 0, tm), wd_ref[cols, :])
        if c == 0:
            acc_s[...] = part
        else:
            acc_s[...] += part

    c = n_chunks - 1
    cols = slice(c * tf, (c + 1) * tf)
    for r in range(n_sub):
        rows = slice(r * rs, (r + 1) * rs)
        f = acc_s[rows, :] + _dot(gate_rows(c, r * rs, (r + 1) * rs), wd_ref[cols, :])
        out_ref[rows, :] = hmid[rows, :] + _rms(f, NORM_EPS) * gpost_ref[...]


def _mix_ffn(h, front, yr, ya, wo, gmix, gpre, wg, wv, cw, cb, wd, gpost, tm, tf, n_sub, embed):
    B, TP, _ = yr.shape
    assert D_FF // tf >= 2 and tm % (n_sub * BF16_ROWS) == 0
    rows = lambda width: pl.BlockSpec((None, tm, width), lambda b, i: (b, i, 0))
    once = pl.Buffered(1)
    const = lambda shape: pl.BlockSpec(shape, lambda b, i: (0, 0), pipeline_mode=once)
    vec = _full((1, D_MODEL))
    return pl.pallas_call(
        functools.partial(_mix_ffn_kernel, tf=tf, n_sub=n_sub, embed=embed),
        grid=(B, TP // tm),
        in_specs=[_residual_spec(tm, embed, lambda b, i: (b, i)), _full((FRONT, D_MODEL)),
                  rows(LRU_WIDTH), rows(ATT_WIDTH), const((D_MODEL, D_MODEL)),
                  vec, vec, const((D_MODEL, D_FF)), const((D_MODEL, D_FF)),
                  _full((FFN_CONV, D_FF)), _full((1, D_FF)), const((D_FF, D_MODEL)), vec],
        out_specs=rows(D_MODEL),
        out_shape=jax.ShapeDtypeStruct((B, TP, D_MODEL), F32),
        scratch_shapes=[pltpu.VMEM((tm + BF16_ROWS, D_MODEL), BF16),
                        pltpu.VMEM((tm, D_MODEL), F32),
                        pltpu.VMEM((tm + BF16_ROWS, tf), F32),
                        pltpu.VMEM((tm + BF16_ROWS, tf), F32),
                        pltpu.VMEM((tm, tf), F32), pltpu.VMEM((tm, tf), F32),
                        pltpu.VMEM((tm, D_MODEL), F32)],
        compiler_params=_params("parallel", "arbitrary"),
        name="mix_ffn",
    )(h, front, yr, ya, wo, gmix, gpre, wg, wv, cw, cb, wd, gpost)


def _block_diag(w):
    per = GATE_TILE // (LRU_WIDTH // LRU_BLOCKS)
    bd = w.shape[-1]
    w = w.reshape(LRU_WIDTH // GATE_TILE, per, bd, bd)
    eye = jnp.eye(per, dtype=w.dtype)
    out = jnp.einsum('jpcd,pq->jpcqd', w, eye)
    return out.reshape(LRU_WIDTH // GATE_TILE, GATE_TILE, GATE_TILE).astype(BF16)


def _keep_rows(tt):
    p = jnp.arange(tt)
    orig = (p % SUBLANES) * (tt // SUBLANES) + p // SUBLANES
    keep = jnp.stack([(orig >= PAD).astype(F32), jnp.ones(tt, F32)])
    return jnp.broadcast_to(keep[:, :, None], (2, tt, LANES))


def _rope_tables(seq):
    inv = 1.0 / (ROPE_THETA ** (jnp.arange(0, HEAD_DIM, 2, dtype=F32) / HEAD_DIM))
    pos = jnp.concatenate([jnp.zeros(PAD, jnp.int32), jnp.arange(N_META + seq)]).astype(F32)
    ang = pos[:, None] * inv[None, :]
    cos, sin = jnp.cos(ang), jnp.sin(ang)
    reps = LANES // HEAD_DIM
    return (jnp.tile(jnp.concatenate([cos, cos], axis=1), (1, reps)),
            jnp.tile(jnp.concatenate([-sin, sin], axis=1), (1, reps)))


def kernel(x, meta_tokens, ln_mix_pre, ln_mix_post, ln_ffn_pre, ln_ffn_post, w_in, lru_conv_w, lru_conv_b, lru_wa, lru_ba, lru_wx, lru_bx, lru_a_param, lru_out_norm, lam_q1, lam_k1, lam_q2, lam_k2, diff_subln, w_out, w_up, ffn_conv_w, ffn_conv_b, w_down):
    B, S, _ = x.shape
    depth = w_in.shape[0]
    TP = FRONT + S
    assert TP % 528 == 0 and S % 512 == 0
    front = jnp.concatenate([jnp.zeros((PAD, D_MODEL), x.dtype), meta_tokens.astype(x.dtype)])
    h = x
    cos, sin = _rope_tables(S)
    vec = lambda a: a.reshape(1, -1)

    for l in range(depth):
        lambda_init = 0.8 - 0.6 * math.exp(-0.3 * l)
        q, k, v, yr = _in_lru(h, front, vec(ln_mix_pre[l]), w_in[l].astype(BF16), cos, sin, _keep_rows(528),
                              lru_conv_w[l], vec(lru_conv_b[l]),
                              _block_diag(lru_wa[l]), vec(lru_ba[l]), _block_diag(lru_wx[l]),
                              vec(lru_bx[l]), vec(lru_a_param[l]), vec(lru_out_norm[l]),
                              tm=528, n_sub=3, embed=(l == 0))
        lam = jnp.stack([lam_q1[l], lam_k1[l], lam_q2[l], lam_k2[l]])
        ya = _attn(q, k, v, lam, vec(diff_subln[l]), lambda_init, tq=512)
        wu = w_up[l].astype(BF16)
        h = _mix_ffn(h, front, yr, ya, w_out[l].astype(BF16), vec(ln_mix_post[l]), vec(ln_ffn_pre[l]),
                     wu[:, :D_FF], wu[:, D_FF:], ffn_conv_w[l], vec(ffn_conv_b[l]),
                     w_down[l].astype(BF16), vec(ln_ffn_post[l]), tm=528, tf=512, n_sub=3,
                     embed=(l == 0))
    return h[:, FRONT:]
```

```python
import functools
import math

import jax
import jax.numpy as jnp
from jax import lax
from jax.experimental import pallas as pl
from jax.experimental.pallas import tpu as pltpu

D_MODEL = 1024
N_META = 16
CHUNK = 64
LRU_WIDTH = 512
LRU_BLOCKS = 8
LRU_CONV = 4
LRU_C = 8.0
ATT_WIDTH = 512
HEADS = 4
HEAD_DIM = 64
V_DIM = 128
ROPE_THETA = 10000.0
D_FF = 3 * D_MODEL
FFN_CONV = 3
NORM_EPS = 1e-6
SUBLN_EPS = 1e-5
IN_WIDTH = 2 * LRU_WIDTH + 3 * ATT_WIDTH
NEG_INF = -1e30

LANES = 128
SUBLANES = 8
BF16_ROWS = 16
GATE_TILE = 256
FRONT = LANES
PAD = FRONT - N_META

VMEM_LIMIT = 56 * 1024 * 1024

F32 = jnp.float32
BF16 = jnp.bfloat16


def _dot(a, b):
    return jnp.dot(a, b, preferred_element_type=F32)


def _dot_t(a, b):
    return lax.dot_general(a, b, (((0,), (0,)), ((), ())), preferred_element_type=F32)


def _gelu(x):
    c = math.sqrt(2.0 / math.pi)
    return 0.5 * x * (1.0 + jnp.tanh(c * (x + 0.044715 * (x * x * x))))


def _sigmoid(x):
    return 1.0 / (1.0 + jnp.exp(-x))


def _rms(x, eps):
    return x * lax.rsqrt(jnp.mean(x * x, axis=-1, keepdims=True) + eps)


def _residual_rows(h_ref, front_ref, first, lo, hi, embed):
    direct = h_ref[lo:hi, :]
    if not embed:
        return direct
    if hi <= FRONT:
        shifted = front_ref[lo:hi, :]
    elif lo >= FRONT:
        shifted = h_ref[lo - FRONT:hi - FRONT, :]
    else:
        shifted = jnp.concatenate([front_ref[lo:FRONT, :], h_ref[0:hi - FRONT, :]], axis=0)
    return jnp.where(first, shifted, direct)


def _residual_spec(tm, embed, tile):
    if not embed:
        return pl.BlockSpec((None, tm, D_MODEL), lambda *ids: (*tile(*ids), 0))

    def window(*ids):
        b, i = tile(*ids)
        return b, pl.multiple_of(jnp.maximum(i * tm - FRONT, 0), SUBLANES), 0

    return pl.BlockSpec((None, pl.Element(tm), pl.Element(D_MODEL)), window)


def _params(*sem):
    return pltpu.CompilerParams(dimension_semantics=sem, vmem_limit_bytes=VMEM_LIMIT)


def _full(shape):
    n = len(shape)
    return pl.BlockSpec(shape, lambda *_: (0,) * n)


def _in_lru_kernel(h_ref, front_ref, g_ref, w_ref, cos_ref, sin_ref, keep_ref, cw_ref, cb_ref,
                   wa_ref, ba_ref, wx_ref, bx_ref, ap_ref, gn_ref,
                   q_ref, k_ref, v_ref, yr_ref,
                   u_s, xr_s, gr_s, a_s, b_s, h_s, tails, carry, *, nt, n_sub, embed):
    t = pl.program_id(0)
    tm = h_ref.shape[0]
    n_slabs = xr_s.shape[0]
    seg = tm // SUBLANES
    rs = tm // n_sub
    sub = lax.broadcasted_iota(jnp.int32, (SUBLANES, LRU_WIDTH), 0)
    fresh = ((t + nt - 1) % nt) == 0
    lead = (jnp.minimum(t, pl.num_programs(0) - 2) % nt) == 0

    @pl.when(t == 0)
    def _():
        xr_s[...] = jnp.zeros_like(xr_s)
        gr_s[...] = jnp.zeros_like(gr_s)
        tails[...] = jnp.zeros_like(tails)
        carry[...] = jnp.zeros_like(carry)

    def permuted(j):
        return jnp.concatenate([xr_s[c, pl.ds(j, SUBLANES, stride=seg), :]
                                for c in range(n_slabs)], axis=1)

    xp = [permuted(j) for j in range(seg)]
    taps = LRU_CONV - 1
    heads = []
    for k in range(taps):
        prev = jnp.where(fresh, 0.0, tails[k * SUBLANES:(k + 1) * SUBLANES, :])
        heads.append(jnp.where(sub == 0, pltpu.roll(prev, 1, 0),
                               pltpu.roll(xp[seg - taps + k], 1, 0)))
    for k in range(taps):
        tails[k * SUBLANES:(k + 1) * SUBLANES, :] = xp[seg - taps + k]
    ext = jnp.concatenate(heads + xp, axis=0)
    xc = cb_ref[...]
    for k in range(LRU_CONV):
        xc = xc + ext[k * SUBLANES:k * SUBLANES + tm, :] * cw_ref[k:k + 1, :]
    xcb = xc.astype(BF16)

    for r in range(n_sub):
        rows = slice(r * rs, (r + 1) * rs)
        hr = _residual_rows(h_ref, front_ref, lead, r * rs, (r + 1) * rs, embed)
        u_s[rows, :] = (_rms(hr, NORM_EPS) * g_ref[...]).astype(BF16)
    o = 2 * LRU_WIDTH
    zq = jnp.concatenate([_dot(u_s[r * rs:(r + 1) * rs, :], w_ref[:, o:o + ATT_WIDTH])
                          for r in range(n_sub)], axis=0)
    zk = _dot(u_s[...], w_ref[:, o + ATT_WIDTH:o + 2 * ATT_WIDTH])
    v_ref[...] = _dot(u_s[...], w_ref[:, o + 2 * ATT_WIDTH:o + 3 * ATT_WIDTH]).astype(BF16)
    zx = _dot(u_s[...], w_ref[:, 0:LRU_WIDTH])
    for c in range(n_slabs):
        xr_s[c] = zx[:, c * LANES:(c + 1) * LANES]

    def gate(wg_ref, b_ref):
        parts = [_dot(xcb[:, j * GATE_TILE:(j + 1) * GATE_TILE], wg_ref[j])
                 for j in range(LRU_WIDTH // GATE_TILE)]
        return _sigmoid(jnp.concatenate(parts, axis=1) + b_ref[...])

    r_gate = gate(wa_ref, ba_ref)
    i_gate = gate(wx_ref, bx_ref)

    ap = ap_ref[...]
    log_sig = jnp.minimum(ap, 0.0) - jnp.log1p(jnp.exp(-jnp.abs(ap)))
    a = jnp.exp2(r_gate * (LRU_C * math.log2(math.e) * log_sig))
    a_s[...] = a
    d = 1.0 - a * a
    b = jnp.where(d > 0.0, d * lax.rsqrt(d), 0.0) * (i_gate * xc)
    keep = keep_ref[...]
    b_s[...] = b * jnp.concatenate([keep] * n_slabs, axis=1)

    vreg = lambda ref, j: ref[j * SUBLANES:(j + 1) * SUBLANES, :]

    h_end = jnp.zeros((SUBLANES, LRU_WIDTH), F32)
    a_prod = jnp.ones((SUBLANES, LRU_WIDTH), F32)
    for j in range(seg):
        aj = vreg(a_s, j)
        h_end = aj * h_end + vreg(b_s, j)
        a_prod = aj * a_prod

    c_in = jnp.where(fresh, 0.0, carry[...])
    enter = c_in
    for _ in range(SUBLANES - 1):
        enter = jnp.where(sub == 0, c_in, pltpu.roll(h_end + a_prod * enter, 1, 0))
    last = h_end + a_prod * enter
    carry[...] = jnp.broadcast_to(last[SUBLANES - 1:SUBLANES, :], (SUBLANES, LRU_WIDTH))

    hv = enter
    for j in range(seg):
        hv = vreg(a_s, j) * hv + vreg(b_s, j)
        for c in range(n_slabs):
            h_s[c, pl.ds(j, SUBLANES, stride=seg), :] = hv[:, c * LANES:(c + 1) * LANES]

    hn = jnp.concatenate([h_s[c] for c in range(n_slabs)], axis=1)
    y = hn * _gelu(gr_s[...])
    yr_ref[...] = (_rms(y, NORM_EPS) * gn_ref[...]).astype(BF16)

    gr_s[...] = _dot(u_s[...], w_ref[:, LRU_WIDTH:2 * LRU_WIDTH])

    cos = cos_ref[...]
    sin = sin_ref[...]
    lane = lax.broadcasted_iota(jnp.int32, (tm, LANES), 1)
    low_half = (lane & (HEAD_DIM // 2)) == 0

    def rope(x):
        up = pltpu.roll(x, LANES - HEAD_DIM // 2, 1)
        down = pltpu.roll(x, HEAD_DIM // 2, 1)
        return x * cos + jnp.where(low_half, up, down) * sin

    q_scale = HEAD_DIM ** -0.5 * math.log2(math.e)
    for hd in range(HEADS):
        sl = slice(hd * LANES, (hd + 1) * LANES)
        q_ref[:, sl] = (rope(zq[:, sl]) * q_scale).astype(BF16)
        k_ref[:, sl] = rope(zk[:, sl]).astype(BF16)


def _in_lru(h, front, g, layer, w, cos, sin, keep, cw, cb, wa, ba, wx, bx, ap, gn, tm, n_sub, embed):
    B = h.shape[0]
    TP = h.shape[1] + (FRONT if embed else 0)
    nt = TP // tm
    last = B * nt - 1
    assert tm % (n_sub * BF16_ROWS) == 0
    proj = lambda t: jnp.minimum(t, last)
    lru = lambda t: jnp.maximum(t - 1, 0)
    rows = lambda width, tile: pl.BlockSpec(
        (None, tm, width), lambda t: (tile(t) // nt, tile(t) % nt, 0))
    out = lambda width: jax.ShapeDtypeStruct((B, TP, width), BF16)
    vec = _full((1, LRU_WIDTH))
    wspec = _full((LRU_WIDTH // GATE_TILE, GATE_TILE, GATE_TILE))
    n_slabs = LRU_WIDTH // LANES
    taps = LRU_CONV - 1
    return pl.pallas_call(
        functools.partial(_in_lru_kernel, nt=nt, n_sub=n_sub, embed=embed),
        grid=(B * nt + 1,),
        in_specs=[_residual_spec(tm, embed, lambda t: (proj(t) // nt, proj(t) % nt)),
                  _full((FRONT, D_MODEL)), _full((1, D_MODEL)),
                  pl.BlockSpec((None, D_MODEL, IN_WIDTH), lambda t: (layer, 0, 0),
                               pipeline_mode=pl.Buffered(1)),
                  pl.BlockSpec((tm, LANES), lambda t: (proj(t) % nt, 0)),
                  pl.BlockSpec((tm, LANES), lambda t: (proj(t) % nt, 0)),
                  pl.BlockSpec((None, tm, LANES), lambda t: (jnp.minimum(lru(t) % nt, 1), 0, 0)),
                  _full((LRU_CONV, LRU_WIDTH)), vec, wspec, vec, wspec, vec, vec, vec],
        out_specs=[rows(ATT_WIDTH, proj), rows(ATT_WIDTH, proj), rows(ATT_WIDTH, proj),
                   rows(LRU_WIDTH, lru)],
        out_shape=[out(ATT_WIDTH), out(ATT_WIDTH), out(ATT_WIDTH), out(LRU_WIDTH)],
        scratch_shapes=[pltpu.VMEM((tm, D_MODEL), BF16),
                        pltpu.VMEM((n_slabs, tm, LANES), F32),
                        pltpu.VMEM((tm, LRU_WIDTH), F32),
                        pltpu.VMEM((tm, LRU_WIDTH), F32),
                        pltpu.VMEM((tm, LRU_WIDTH), F32),
                        pltpu.VMEM((n_slabs, tm, LANES), F32),
                        pltpu.VMEM((taps * SUBLANES, LRU_WIDTH), F32),
                        pltpu.VMEM((SUBLANES, LRU_WIDTH), F32)],
        compiler_params=_params("arbitrary"),
        name="in_lru",
    )(h, front, g, w, cos, sin, keep, cw, cb, wa, ba, wx, bx, ap, gn)


def _attn_kernel(q_ref, k_ref, v_ref, lam_ref, sub_ref, out_ref,
                 qc_s, s0, s1, x0, x1, m_s, l_s, acc_s, *, lambda_init, tq):
    n_tiles = (q_ref.shape[0] - FRONT) // tq
    lp = lam_ref[...]
    lam = (jnp.exp(jnp.sum(lp[0:1] * lp[1:2], axis=1, keepdims=True))
           - jnp.exp(jnp.sum(lp[2:3] * lp[3:4], axis=1, keepdims=True)) + lambda_init)
    gain = sub_ref[...] * (1.0 - lambda_init)
    k_meta = k_ref[PAD:FRONT, :]
    v_meta = v_ref[PAD:FRONT, :]

    def qcat(q):
        qt = q.T
        dim = lax.broadcasted_iota(jnp.int32, qt.shape, 0)
        zero = jnp.zeros_like(qt)
        return jnp.concatenate([jnp.where(dim < HEAD_DIM, qt, zero),
                                jnp.where(dim >= HEAD_DIM, qt, zero)], axis=1)

    def meta_state(qc):
        s = _dot(k_meta, qc)
        m = jnp.max(s, axis=0, keepdims=True)
        p = jnp.exp2(s - m)
        return m, jnp.sum(p, axis=0, keepdims=True), _dot_t(v_meta, p.astype(BF16))

    def finish(acc, l, rows):
        n = acc.shape[1] // 2
        o = acc * (1.0 / l)
        o = o[:, :n] - lam * o[:, n:]
        o = o * lax.rsqrt(jnp.mean(o * o, axis=0, keepdims=True) + SUBLN_EPS)
        out_ref[rows, :] = (o.T * gain).astype(BF16)

    def scores(s_ref, mx_ref, j):
        start = pl.multiple_of(FRONT + j * tq, LANES)
        kb = jnp.concatenate([k_ref[pl.ds(start, tq), :], k_meta], axis=0)
        s = _dot(kb, qc_s[...])
        s_ref[...] = s
        mx_ref[...] = jnp.max(s[0:tq], axis=0, keepdims=True)

    def absorb(s_ref, mx_ref, j, diagonal):
        vb = v_ref[pl.ds(pl.multiple_of(FRONT + j * tq, LANES), tq), :]
        if diagonal:
            s = s_ref[...]
            row = lax.broadcasted_iota(jnp.int32, (tq + N_META, 1), 0)
            kc = jnp.where(row < tq, row // CHUNK, 0)
            qc_ = (lax.broadcasted_iota(jnp.int32, (1, 2 * tq), 1) % tq) // CHUNK
            s = jnp.where(kc <= qc_, s, NEG_INF)
            mx = jnp.max(s, axis=0, keepdims=True)
            vb = jnp.concatenate([vb, v_meta], axis=0)
        else:
            s = s_ref[0:tq, :]
            mx = mx_ref[...]
        m_old = m_s[...]
        m_new = jnp.maximum(m_old, mx)
        alpha = jnp.exp2(m_old - m_new)
        p = jnp.exp2(s - m_new)
        l_s[...] = alpha * l_s[...] + jnp.sum(p, axis=0, keepdims=True)
        m_s[...] = m_new
        acc_s[...] = alpha * acc_s[...] + _dot_t(vb, p.astype(BF16))

    m, l, acc = meta_state(qcat(q_ref[0:FRONT, :]))
    finish(acc, l, slice(0, FRONT))
    out_ref[0:PAD, :] = jnp.zeros((PAD, V_DIM), BF16)

    def load_queries(i):
        qc_s[...] = qcat(q_ref[pl.ds(pl.multiple_of(FRONT + i * tq, LANES), tq), :])

    load_queries(0)
    scores(s0, x0, 0)

    def tile(i, c):
        m_s[...] = jnp.full_like(m_s, NEG_INF)
        l_s[...] = jnp.zeros_like(l_s)
        acc_s[...] = jnp.zeros_like(acc_s)

        def pair(jj, c2):
            j = 2 * jj
            scores(s1, x1, j + 1)
            absorb(s0, x0, j, False)
            scores(s0, x0, j + 2)
            absorb(s1, x1, j + 1, False)
            return c2

        lax.fori_loop(0, i // 2, pair, 0)

        @pl.when(i % 2 == 1)
        def _():
            scores(s1, x1, i)
            absorb(s0, x0, i - 1, False)
            absorb(s1, x1, i, True)

        @pl.when(i % 2 == 0)
        def _():
            absorb(s0, x0, i, True)

        acc, l = acc_s[...], l_s[...]
        load_queries(jnp.minimum(i + 1, n_tiles - 1))
        scores(s0, x0, 0)
        finish(acc, l, pl.ds(pl.multiple_of(FRONT + i * tq, LANES), tq))
        return c

    lax.fori_loop(0, n_tiles, tile, 0)


def _attn(q, k, v, lam, sub, lambda_init, tq):
    B, TP, _ = k.shape
    kern = functools.partial(_attn_kernel, lambda_init=lambda_init, tq=tq)
    return pl.pallas_call(
        kern,
        grid=(B, HEADS),
        in_specs=[pl.BlockSpec((None, TP, V_DIM), lambda b, h: (b, 0, h)),
                  pl.BlockSpec((None, TP, V_DIM), lambda b, h: (b, 0, h)),
                  pl.BlockSpec((None, TP, V_DIM), lambda b, h: (b, 0, h)),
                  _full((4, HEAD_DIM)), _full((1, V_DIM))],
        out_specs=pl.BlockSpec((None, TP, V_DIM), lambda b, h: (b, 0, h)),
        out_shape=jax.ShapeDtypeStruct((B, TP, ATT_WIDTH), BF16),
        scratch_shapes=[pltpu.VMEM((V_DIM, 2 * tq), BF16),
                        pltpu.VMEM((tq + N_META, 2 * tq), F32),
                        pltpu.VMEM((tq + N_META, 2 * tq), F32),
                        pltpu.VMEM((1, 2 * tq), F32), pltpu.VMEM((1, 2 * tq), F32),
                        pltpu.VMEM((1, 2 * tq), F32), pltpu.VMEM((1, 2 * tq), F32),
                        pltpu.VMEM((V_DIM, 2 * tq), F32)],
        compiler_params=_params("parallel", "parallel"),
        name="diff_attn",
    )(q, k, v, lam, sub)


def _mix_ffn_kernel(h_ref, front_ref, yr_ref, ya_ref, wo_ref, gmix_ref, gpre_ref, wu_ref,
                    cw_ref, cb_ref, wd_ref, gpost_ref, out_ref,
                    uext, hmid, g_a, g_b, v_a, v_b, acc_s, *, tf, n_sub, n_pro, embed):
    tm = out_ref.shape[0]
    win = h_ref.shape[0]
    halo = BF16_ROWS
    lead = win - tm
    rs = tm // n_sub
    ps = win // n_pro
    first = pl.program_id(1) == 0
    n_chunks = D_FF // tf
    bufs = ((g_a, v_a), (g_b, v_b))

    if lead == 0:
        @pl.when(first)
        def _():
            uext[0:halo, :] = jnp.zeros((halo, D_MODEL), BF16)

        @pl.when(jnp.logical_not(first))
        def _():
            uext[0:halo, :] = uext[tm:tm + halo, :]

    ys = [_dot(yr_ref[r * ps:(r + 1) * ps, :], wo_ref[0:LRU_WIDTH, :])
          + _dot(ya_ref[r * ps:(r + 1) * ps, :], wo_ref[LRU_WIDTH:, :]) for r in range(n_pro)]
    for r in range(n_pro):
        lo, hi = r * ps, (r + 1) * ps
        hm = (_residual_rows(h_ref, front_ref, first, lo, hi, embed)
              + _rms(ys[r], NORM_EPS) * gmix_ref[...])
        tlo = max(lo - lead, 0)
        hmid[tlo:hi - lead, :] = hm[tlo + lead - lo:, :]
        uext[halo - lead + lo:halo - lead + hi, :] = (
            _rms(hm, NORM_EPS) * gpre_ref[...]).astype(BF16)

    def up_rows(c, lo, hi):
        g_s, v_s = bufs[c % 2]
        cols = slice(c * tf, (c + 1) * tf)
        vcols = slice(D_FF + c * tf, D_FF + (c + 1) * tf)
        glo = 0 if lo == 0 else halo + lo
        g_s[glo:halo + hi, :] = _dot(uext[glo:halo + hi, :], wu_ref[:, cols])
        v_s[lo:hi, :] = _dot(uext[halo + lo:halo + hi, :], wu_ref[:, vcols])

    def gate_rows(c, lo, hi):
        g_s, v_s = bufs[c % 2]
        cols = slice(c * tf, (c + 1) * tf)
        gc = cb_ref[:, cols]
        for k in range(FFN_CONV):
            off = halo - (FFN_CONV - 1) + k
            gc = gc + g_s[off + lo:off + hi, :] * cw_ref[k:k + 1, cols]
        return (_gelu(gc) * v_s[lo:hi, :]).astype(BF16)

    for r in range(n_sub):
        up_rows(0, r * rs, (r + 1) * rs)

    for c in range(n_chunks - 1):
        cols = slice(c * tf, (c + 1) * tf)
        up_rows(c + 1, 0, tm)
        part = _dot(gate_rows(c, 0, tm), wd_ref[cols, :])
        if c == 0:
            acc_s[...] = part
        else:
            acc_s[...] += part

    c = n_chunks - 1
    cols = slice(c * tf, (c + 1) * tf)
    for r in range(n_sub):
        rows = slice(r * rs, (r + 1) * rs)
        f = acc_s[rows, :] + _dot(gate_rows(c, r * rs, (r + 1) * rs), wd_ref[cols, :])
        out_ref[rows, :] = hmid[rows, :] + _rms(f, NORM_EPS) * gpost_ref[...]


def _mix_ffn(h, front, yr, ya, layer, wo, gmix, gpre, wu, cw, cb, wd, gpost,
             tm, tf, n_sub, n_pro, embed, strip):
    B, TP, _ = yr.shape
    assert D_FF // tf >= 2 and tm % (n_sub * BF16_ROWS) == 0 and not (embed and strip)
    once = pl.Buffered(1)
    const = lambda shape: pl.BlockSpec((None,) + shape, lambda b, i: (layer, 0, 0),
                                       pipeline_mode=once)
    vec = _full((1, D_MODEL))
    if strip:
        win = tm + BF16_ROWS
        rows = lambda width: pl.BlockSpec(
            (None, pl.Element(win), pl.Element(width)),
            lambda b, i: (b, pl.multiple_of(FRONT - BF16_ROWS + i * tm, BF16_ROWS), 0))
        h_spec = rows(D_MODEL)
        n_rows = TP - FRONT
    else:
        win = tm
        rows = lambda width: pl.BlockSpec((None, tm, width), lambda b, i: (b, i, 0))
        h_spec = _residual_spec(tm, embed, lambda b, i: (b, i))
        n_rows = TP
    assert win % (n_pro * BF16_ROWS) == 0
    return pl.pallas_call(
        functools.partial(_mix_ffn_kernel, tf=tf, n_sub=n_sub, n_pro=n_pro, embed=embed),
        grid=(B, n_rows // tm),
        in_specs=[h_spec, _full((FRONT, D_MODEL)),
                  rows(LRU_WIDTH), rows(ATT_WIDTH), const((D_MODEL, D_MODEL)),
                  vec, vec, const((D_MODEL, 2 * D_FF)),
                  _full((FFN_CONV, D_FF)), _full((1, D_FF)), const((D_FF, D_MODEL)), vec],
        out_specs=pl.BlockSpec((None, tm, D_MODEL), lambda b, i: (b, i, 0)),
        out_shape=jax.ShapeDtypeStruct((B, n_rows, D_MODEL), F32),
        scratch_shapes=[pltpu.VMEM((tm + BF16_ROWS, D_MODEL), BF16),
                        pltpu.VMEM((tm, D_MODEL), F32),
                        pltpu.VMEM((tm + BF16_ROWS, tf), F32),
                        pltpu.VMEM((tm + BF16_ROWS, tf), F32),
                        pltpu.VMEM((tm, tf), F32), pltpu.VMEM((tm, tf), F32),
                        pltpu.VMEM((tm, D_MODEL), F32)],
        compiler_params=_params("parallel", "arbitrary"),
        name="mix_ffn",
    )(h, front, yr, ya, wo, gmix, gpre, wu, cw, cb, wd, gpost)


def _block_diag(w):
    per = GATE_TILE // (LRU_WIDTH // LRU_BLOCKS)
    bd = w.shape[-1]
    w = w.reshape(LRU_WIDTH // GATE_TILE, per, bd, bd)
    eye = jnp.eye(per, dtype=w.dtype)
    out = jnp.einsum('jpcd,pq->jpcqd', w, eye)
    return out.reshape(LRU_WIDTH // GATE_TILE, GATE_TILE, GATE_TILE).astype(BF16)


def _keep_rows(tt):
    p = jnp.arange(tt)
    orig = (p % SUBLANES) * (tt // SUBLANES) + p // SUBLANES
    keep = jnp.stack([(orig >= PAD).astype(F32), jnp.ones(tt, F32)])
    return jnp.broadcast_to(keep[:, :, None], (2, tt, LANES))


def _rope_tables(seq):
    inv = 1.0 / (ROPE_THETA ** (jnp.arange(0, HEAD_DIM, 2, dtype=F32) / HEAD_DIM))
    pos = jnp.concatenate([jnp.zeros(PAD, jnp.int32), jnp.arange(N_META + seq)]).astype(F32)
    ang = pos[:, None] * inv[None, :]
    cos, sin = jnp.cos(ang), jnp.sin(ang)
    reps = LANES // HEAD_DIM
    return (jnp.tile(jnp.concatenate([cos, cos], axis=1), (1, reps)),
            jnp.tile(jnp.concatenate([-sin, sin], axis=1), (1, reps)))


def kernel(x, meta_tokens, ln_mix_pre, ln_mix_post, ln_ffn_pre, ln_ffn_post, w_in, lru_conv_w, lru_conv_b, lru_wa, lru_ba, lru_wx, lru_bx, lru_a_param, lru_out_norm, lam_q1, lam_k1, lam_q2, lam_k2, diff_subln, w_out, w_up, ffn_conv_w, ffn_conv_b, w_down):
    B, S, _ = x.shape
    depth = w_in.shape[0]
    TP = FRONT + S
    assert TP % 528 == 0 and S % 512 == 0
    front = jnp.concatenate([jnp.zeros((PAD, D_MODEL), x.dtype), meta_tokens.astype(x.dtype)])
    h = x
    cos, sin = _rope_tables(S)
    vec = lambda a: a.reshape(1, -1)

    w_in, w_out, w_up, w_down = (w.astype(BF16) for w in (w_in, w_out, w_up, w_down))
    for l in range(depth):
        lambda_init = 0.8 - 0.6 * math.exp(-0.3 * l)
        final = l == depth - 1
        q, k, v, yr = _in_lru(h, front, vec(ln_mix_pre[l]), l, w_in, cos, sin, _keep_rows(528),
                              lru_conv_w[l], vec(lru_conv_b[l]),
                              _block_diag(lru_wa[l]), vec(lru_ba[l]), _block_diag(lru_wx[l]),
                              vec(lru_bx[l]), vec(lru_a_param[l]), vec(lru_out_norm[l]),
                              tm=528, n_sub=3, embed=(l == 0))
        lam = jnp.stack([lam_q1[l], lam_k1[l], lam_q2[l], lam_k2[l]])
        ya = _attn(q, k, v, lam, vec(diff_subln[l]), lambda_init, tq=512)
        h = _mix_ffn(h, front, yr, ya, l, w_out, vec(ln_mix_post[l]), vec(ln_ffn_pre[l]),
                     w_up, ffn_conv_w[l], vec(ffn_conv_b[l]), w_down, vec(ln_ffn_post[l]),
                     tm=512 if final else 528, tf=512, n_sub=4 if final else 3, n_pro=3,
                     embed=(l == 0), strip=final)
    return h
```

```python
import functools
import math

import jax
import jax.numpy as jnp
from jax import lax
from jax.experimental import pallas as pl
from jax.experimental.pallas import tpu as pltpu

D_MODEL = 1024
N_META = 16
CHUNK = 64
LRU_WIDTH = 512
LRU_BLOCKS = 8
LRU_CONV = 4
LRU_C = 8.0
ATT_WIDTH = 512
HEADS = 4
HEAD_DIM = 64
V_DIM = 128
ROPE_THETA = 10000.0
D_FF = 3 * D_MODEL
FFN_CONV = 3
NORM_EPS = 1e-6
SUBLN_EPS = 1e-5
IN_WIDTH = 2 * LRU_WIDTH + 3 * ATT_WIDTH
NEG_INF = -1e30

LANES = 128
SUBLANES = 8
BF16_ROWS = 16
GATE_TILE = 256
FRONT = LANES
PAD = FRONT - N_META

VMEM_LIMIT = 56 * 1024 * 1024

F32 = jnp.float32
BF16 = jnp.bfloat16


def _dot(a, b):
    return jnp.dot(a, b, preferred_element_type=F32)


def _dot_t(a, b):
    return lax.dot_general(a, b, (((0,), (0,)), ((), ())), preferred_element_type=F32)


def _gelu(x):
    c = math.sqrt(2.0 / math.pi)
    return 0.5 * x * (1.0 + jnp.tanh(c * (x + 0.044715 * (x * x * x))))


def _sigmoid(x):
    return 1.0 / (1.0 + jnp.exp(-x))


def _rms(x, eps):
    return x * lax.rsqrt(jnp.mean(x * x, axis=-1, keepdims=True) + eps)


def _residual_rows(h_ref, front_ref, first, lo, hi, shift):
    direct = h_ref[lo:hi, :]
    base = FRONT - shift
    if hi <= shift:
        shifted = front_ref[base + lo:base + hi, :]
    elif lo >= shift:
        shifted = h_ref[lo - shift:hi - shift, :]
    else:
        shifted = jnp.concatenate([front_ref[base + lo:FRONT, :], h_ref[0:hi - shift, :]], axis=0)
    return jnp.where(first, shifted, direct)


def _frames_window(win, width, align, start):
    def index(*ids):
        b, row = start(*ids)
        return b, pl.multiple_of(row, align), 0

    return pl.BlockSpec((None, pl.Element(win), pl.Element(width)), index)


def _params(*sem):
    return pltpu.CompilerParams(dimension_semantics=sem, vmem_limit_bytes=VMEM_LIMIT)


def _full(shape):
    n = len(shape)
    return pl.BlockSpec(shape, lambda *_: (0,) * n)


def _in_lru_kernel(h_ref, front_ref, g_ref, w_ref, cos_ref, sin_ref, keep_ref, cw_ref, cb_ref,
                   wa_ref, ba_ref, wx_ref, bx_ref, ap_ref, gn_ref,
                   q_ref, k_ref, v_ref, yr_ref,
                   u_s, xr_s, gr_s, a_s, b_s, h_s, tails, carry, *, nt, n_sub):
    t = pl.program_id(0)
    tm = h_ref.shape[0]
    n_slabs = xr_s.shape[0]
    seg = tm // SUBLANES
    rs = tm // n_sub
    sub = lax.broadcasted_iota(jnp.int32, (SUBLANES, LRU_WIDTH), 0)
    fresh = ((t + nt - 1) % nt) == 0
    lead = (jnp.minimum(t, pl.num_programs(0) - 2) % nt) == 0

    @pl.when(t == 0)
    def _():
        xr_s[...] = jnp.zeros_like(xr_s)
        gr_s[...] = jnp.zeros_like(gr_s)
        tails[...] = jnp.zeros_like(tails)
        carry[...] = jnp.zeros_like(carry)

    def permuted(j):
        return jnp.concatenate([xr_s[c, pl.ds(j, SUBLANES, stride=seg), :]
                                for c in range(n_slabs)], axis=1)

    xp = [permuted(j) for j in range(seg)]
    taps = LRU_CONV - 1
    heads = []
    for k in range(taps):
        prev = jnp.where(fresh, 0.0, tails[k * SUBLANES:(k + 1) * SUBLANES, :])
        heads.append(jnp.where(sub == 0, pltpu.roll(prev, 1, 0),
                               pltpu.roll(xp[seg - taps + k], 1, 0)))
    for k in range(taps):
        tails[k * SUBLANES:(k + 1) * SUBLANES, :] = xp[seg - taps + k]
    ext = jnp.concatenate(heads + xp, axis=0)
    xc = cb_ref[...]
    for k in range(LRU_CONV):
        xc = xc + ext[k * SUBLANES:k * SUBLANES + tm, :] * cw_ref[k:k + 1, :]
    xcb = xc.astype(BF16)

    for r in range(n_sub):
        rows = slice(r * rs, (r + 1) * rs)
        hr = _residual_rows(h_ref, front_ref, lead, r * rs, (r + 1) * rs, FRONT)
        u_s[rows, :] = (_rms(hr, NORM_EPS) * g_ref[...]).astype(BF16)
    o = 2 * LRU_WIDTH
    zq = jnp.concatenate([_dot(u_s[r * rs:(r + 1) * rs, :], w_ref[:, o:o + ATT_WIDTH])
                          for r in range(n_sub)], axis=0)
    zk = _dot(u_s[...], w_ref[:, o + ATT_WIDTH:o + 2 * ATT_WIDTH])
    v_ref[...] = _dot(u_s[...], w_ref[:, o + 2 * ATT_WIDTH:o + 3 * ATT_WIDTH]).astype(BF16)
    zx = _dot(u_s[...], w_ref[:, 0:LRU_WIDTH])
    for c in range(n_slabs):
        xr_s[c] = zx[:, c * LANES:(c + 1) * LANES]

    def gate(wg_ref, b_ref):
        parts = [_dot(xcb[:, j * GATE_TILE:(j + 1) * GATE_TILE], wg_ref[j])
                 for j in range(LRU_WIDTH // GATE_TILE)]
        return _sigmoid(jnp.concatenate(parts, axis=1) + b_ref[...])

    r_gate = gate(wa_ref, ba_ref)
    i_gate = gate(wx_ref, bx_ref)

    ap = ap_ref[...]
    log_sig = jnp.minimum(ap, 0.0) - jnp.log1p(jnp.exp(-jnp.abs(ap)))
    a = jnp.exp2(r_gate * (LRU_C * math.log2(math.e) * log_sig))
    a_s[...] = a
    d = 1.0 - a * a
    b = jnp.where(d > 0.0, d * lax.rsqrt(d), 0.0) * (i_gate * xc)
    keep = keep_ref[...]
    b_s[...] = b * jnp.concatenate([keep] * n_slabs, axis=1)

    vreg = lambda ref, j: ref[j * SUBLANES:(j + 1) * SUBLANES, :]

    h_end = jnp.zeros((SUBLANES, LRU_WIDTH), F32)
    a_prod = jnp.ones((SUBLANES, LRU_WIDTH), F32)
    for j in range(seg):
        aj = vreg(a_s, j)
        h_end = aj * h_end + vreg(b_s, j)
        a_prod = aj * a_prod

    c_in = jnp.where(fresh, 0.0, carry[...])
    enter = c_in
    for _ in range(SUBLANES - 1):
        enter = jnp.where(sub == 0, c_in, pltpu.roll(h_end + a_prod * enter, 1, 0))
    last = h_end + a_prod * enter
    carry[...] = jnp.broadcast_to(last[SUBLANES - 1:SUBLANES, :], (SUBLANES, LRU_WIDTH))

    hv = enter
    for j in range(seg):
        hv = vreg(a_s, j) * hv + vreg(b_s, j)
        for c in range(n_slabs):
            h_s[c, pl.ds(j, SUBLANES, stride=seg), :] = hv[:, c * LANES:(c + 1) * LANES]

    hn = jnp.concatenate([h_s[c] for c in range(n_slabs)], axis=1)
    y = hn * _gelu(gr_s[...])
    yr_ref[...] = (_rms(y, NORM_EPS) * gn_ref[...]).astype(BF16)

    gr_s[...] = _dot(u_s[...], w_ref[:, LRU_WIDTH:2 * LRU_WIDTH])

    cos = cos_ref[...]
    sin = sin_ref[...]
    lane = lax.broadcasted_iota(jnp.int32, (tm, LANES), 1)
    low_half = (lane & (HEAD_DIM // 2)) == 0

    def rope(x):
        up = pltpu.roll(x, LANES - HEAD_DIM // 2, 1)
        down = pltpu.roll(x, HEAD_DIM // 2, 1)
        return x * cos + jnp.where(low_half, up, down) * sin

    q_scale = HEAD_DIM ** -0.5 * math.log2(math.e)
    for hd in range(HEADS):
        sl = slice(hd * LANES, (hd + 1) * LANES)
        q_ref[:, sl] = (rope(zq[:, sl]) * q_scale).astype(BF16)
        k_ref[:, sl] = rope(zk[:, sl]).astype(BF16)


def _in_lru(h, front, g, layer, w, cos, sin, keep, cw, cb, wa, ba, wx, bx, ap, gn, tm, n_sub):
    B = h.shape[0]
    TP = h.shape[1] + FRONT
    nt = TP // tm
    last = B * nt - 1
    assert tm % (n_sub * BF16_ROWS) == 0
    proj = lambda t: jnp.minimum(t, last)
    lru = lambda t: jnp.maximum(t - 1, 0)
    rows = lambda width, tile: pl.BlockSpec(
        (None, tm, width), lambda t: (tile(t) // nt, tile(t) % nt, 0))
    out = lambda width: jax.ShapeDtypeStruct((B, TP, width), BF16)
    vec = _full((1, LRU_WIDTH))
    wspec = _full((LRU_WIDTH // GATE_TILE, GATE_TILE, GATE_TILE))
    n_slabs = LRU_WIDTH // LANES
    taps = LRU_CONV - 1
    return pl.pallas_call(
        functools.partial(_in_lru_kernel, nt=nt, n_sub=n_sub),
        grid=(B * nt + 1,),
        in_specs=[_frames_window(tm, D_MODEL, SUBLANES, lambda t: (
                      proj(t) // nt, jnp.maximum((proj(t) % nt) * tm - FRONT, 0))),
                  pl.BlockSpec((None, FRONT, D_MODEL), lambda t: (proj(t) // nt, 0, 0)),
                  _full((1, D_MODEL)),
                  pl.BlockSpec((None, D_MODEL, IN_WIDTH), lambda t: (layer, 0, 0),
                               pipeline_mode=pl.Buffered(1)),
                  pl.BlockSpec((tm, LANES), lambda t: (proj(t) % nt, 0)),
                  pl.BlockSpec((tm, LANES), lambda t: (proj(t) % nt, 0)),
                  pl.BlockSpec((None, tm, LANES), lambda t: (jnp.minimum(lru(t) % nt, 1), 0, 0)),
                  _full((LRU_CONV, LRU_WIDTH)), vec, wspec, vec, wspec, vec, vec, vec],
        out_specs=[rows(ATT_WIDTH, proj), rows(ATT_WIDTH, proj), rows(ATT_WIDTH, proj),
                   rows(LRU_WIDTH, lru)],
        out_shape=[out(ATT_WIDTH), out(ATT_WIDTH), out(ATT_WIDTH), out(LRU_WIDTH)],
        scratch_shapes=[pltpu.VMEM((tm, D_MODEL), BF16),
                        pltpu.VMEM((n_slabs, tm, LANES), F32),
                        pltpu.VMEM((tm, LRU_WIDTH), F32),
                        pltpu.VMEM((tm, LRU_WIDTH), F32),
                        pltpu.VMEM((tm, LRU_WIDTH), F32),
                        pltpu.VMEM((n_slabs, tm, LANES), F32),
                        pltpu.VMEM((taps * SUBLANES, LRU_WIDTH), F32),
                        pltpu.VMEM((SUBLANES, LRU_WIDTH), F32)],
        compiler_params=_params("arbitrary"),
        name="in_lru",
    )(h, front, g, w, cos, sin, keep, cw, cb, wa, ba, wx, bx, ap, gn)


def _attn_kernel(q_ref, k_ref, v_ref, lam_ref, sub_ref, out_ref,
                 qc_s, s0, s1, x0, x1, m_s, l_s, acc_s, *, lambda_init, tq):
    n_tiles = (q_ref.shape[0] - FRONT) // tq
    lp = lam_ref[...]
    lam = (jnp.exp(jnp.sum(lp[0:1] * lp[1:2], axis=1, keepdims=True))
           - jnp.exp(jnp.sum(lp[2:3] * lp[3:4], axis=1, keepdims=True)) + lambda_init)
    gain = sub_ref[...] * (1.0 - lambda_init)
    k_meta = k_ref[PAD:FRONT, :]
    v_meta = v_ref[PAD:FRONT, :]

    def qcat(q):
        qt = q.T
        dim = lax.broadcasted_iota(jnp.int32, qt.shape, 0)
        zero = jnp.zeros_like(qt)
        return jnp.concatenate([jnp.where(dim < HEAD_DIM, qt, zero),
                                jnp.where(dim >= HEAD_DIM, qt, zero)], axis=1)

    def meta_state(qc):
        s = _dot(k_meta, qc)
        m = jnp.max(s, axis=0, keepdims=True)
        p = jnp.exp2(s - m)
        return m, jnp.sum(p, axis=0, keepdims=True), _dot_t(v_meta, p.astype(BF16))

    def finish(acc, l, rows):
        n = acc.shape[1] // 2
        o = acc * (1.0 / l)
        o = o[:, :n] - lam * o[:, n:]
        o = o * lax.rsqrt(jnp.mean(o * o, axis=0, keepdims=True) + SUBLN_EPS)
        out_ref[rows, :] = (o.T * gain).astype(BF16)

    def scores(s_ref, mx_ref, j):
        start = pl.multiple_of(FRONT + j * tq, LANES)
        kb = jnp.concatenate([k_ref[pl.ds(start, tq), :], k_meta], axis=0)
        s = _dot(kb, qc_s[...])
        s_ref[...] = s
        mx_ref[...] = jnp.max(s[0:tq], axis=0, keepdims=True)

    def absorb(s_ref, mx_ref, j, diagonal):
        vb = v_ref[pl.ds(pl.multiple_of(FRONT + j * tq, LANES), tq), :]
        if diagonal:
            s = s_ref[...]
            row = lax.broadcasted_iota(jnp.int32, (tq + N_META, 1), 0)
            kc = jnp.where(row < tq, row // CHUNK, 0)
            qc_ = (lax.broadcasted_iota(jnp.int32, (1, 2 * tq), 1) % tq) // CHUNK
            s = jnp.where(kc <= qc_, s, NEG_INF)
            mx = jnp.max(s, axis=0, keepdims=True)
            vb = jnp.concatenate([vb, v_meta], axis=0)
        else:
            s = s_ref[0:tq, :]
            mx = mx_ref[...]
        m_old = m_s[...]
        m_new = jnp.maximum(m_old, mx)
        alpha = jnp.exp2(m_old - m_new)
        p = jnp.exp2(s - m_new)
        l_s[...] = alpha * l_s[...] + jnp.sum(p, axis=0, keepdims=True)
        m_s[...] = m_new
        acc_s[...] = alpha * acc_s[...] + _dot_t(vb, p.astype(BF16))

    m, l, acc = meta_state(qcat(q_ref[0:FRONT, :]))
    finish(acc, l, slice(0, FRONT))
    out_ref[0:PAD, :] = jnp.zeros((PAD, V_DIM), BF16)

    def load_queries(i):
        qc_s[...] = qcat(q_ref[pl.ds(pl.multiple_of(FRONT + i * tq, LANES), tq), :])

    load_queries(0)
    scores(s0, x0, 0)

    def tile(i, c):
        m_s[...] = jnp.full_like(m_s, NEG_INF)
        l_s[...] = jnp.zeros_like(l_s)
        acc_s[...] = jnp.zeros_like(acc_s)

        def pair(jj, c2):
            j = 2 * jj
            scores(s1, x1, j + 1)
            absorb(s0, x0, j, False)
            scores(s0, x0, j + 2)
            absorb(s1, x1, j + 1, False)
            return c2

        lax.fori_loop(0, i // 2, pair, 0)

        @pl.when(i % 2 == 1)
        def _():
            scores(s1, x1, i)
            absorb(s0, x0, i - 1, False)
            absorb(s1, x1, i, True)

        @pl.when(i % 2 == 0)
        def _():
            absorb(s0, x0, i, True)

        acc, l = acc_s[...], l_s[...]
        load_queries(jnp.minimum(i + 1, n_tiles - 1))
        scores(s0, x0, 0)
        finish(acc, l, pl.ds(pl.multiple_of(FRONT + i * tq, LANES), tq))
        return c

    lax.fori_loop(0, n_tiles, tile, 0)


def _attn(q, k, v, lam, sub, lambda_init, tq):
    B, TP, _ = k.shape
    kern = functools.partial(_attn_kernel, lambda_init=lambda_init, tq=tq)
    return pl.pallas_call(
        kern,
        grid=(B, HEADS),
        in_specs=[pl.BlockSpec((None, TP, V_DIM), lambda b, h: (b, 0, h)),
                  pl.BlockSpec((None, TP, V_DIM), lambda b, h: (b, 0, h)),
                  pl.BlockSpec((None, TP, V_DIM), lambda b, h: (b, 0, h)),
                  _full((4, HEAD_DIM)), _full((1, V_DIM))],
        out_specs=pl.BlockSpec((None, TP, V_DIM), lambda b, h: (b, 0, h)),
        out_shape=jax.ShapeDtypeStruct((B, TP, ATT_WIDTH), BF16),
        scratch_shapes=[pltpu.VMEM((V_DIM, 2 * tq), BF16),
                        pltpu.VMEM((tq + N_META, 2 * tq), F32),
                        pltpu.VMEM((tq + N_META, 2 * tq), F32),
                        pltpu.VMEM((1, 2 * tq), F32), pltpu.VMEM((1, 2 * tq), F32),
                        pltpu.VMEM((1, 2 * tq), F32), pltpu.VMEM((1, 2 * tq), F32),
                        pltpu.VMEM((V_DIM, 2 * tq), F32)],
        compiler_params=_params("parallel", "parallel"),
        name="diff_attn",
    )(q, k, v, lam, sub)


def _mix_ffn_kernel(h_ref, front_ref, yr_ref, ya_ref, wo_ref, gmix_ref, gpre_ref, wu_ref,
                    cw_ref, cb_ref, wd_ref, gpost_ref, out_ref,
                    uext, hmid, g_a, g_b, v_a, v_b, acc_s, *, tf, n_sub, n_pro, group):
    tm = out_ref.shape[0]
    win = h_ref.shape[0]
    halo = BF16_ROWS
    lead = win - tm
    rs = tm // n_sub
    ps = win // n_pro
    first = pl.program_id(1) == 0
    n_chunks = D_FF // tf
    bufs = ((g_a, v_a), (g_b, v_b))

    if lead == 0:
        uext[0:halo, :] = jnp.zeros((halo, D_MODEL), BF16)

    ys = [_dot(yr_ref[r * ps:(r + 1) * ps, :], wo_ref[0:LRU_WIDTH, :])
          + _dot(ya_ref[r * ps:(r + 1) * ps, :], wo_ref[LRU_WIDTH:, :]) for r in range(n_pro)]
    for r in range(n_pro):
        lo, hi = r * ps, (r + 1) * ps
        hm = (_residual_rows(h_ref, front_ref, first, lo, hi, lead if lead else FRONT)
              + _rms(ys[r], NORM_EPS) * gmix_ref[...])
        tlo = max(lo - lead, 0)
        hmid[tlo:hi - lead, :] = hm[tlo + lead - lo:, :]
        uext[halo - lead + lo:halo - lead + hi, :] = (
            _rms(hm, NORM_EPS) * gpre_ref[...]).astype(BF16)

    def up_rows(c, lo, hi):
        g_s, v_s = bufs[c % 2]
        cols = slice(c * tf, (c + 1) * tf)
        vcols = slice(D_FF + c * tf, D_FF + (c + 1) * tf)
        glo = 0 if lo == 0 else halo + lo
        g_s[glo:halo + hi, :] = _dot(uext[glo:halo + hi, :], wu_ref[:, cols])
        v_s[lo:hi, :] = _dot(uext[halo + lo:halo + hi, :], wu_ref[:, vcols])

    def gate_rows(c, lo, hi):
        g_s, v_s = bufs[c % 2]
        cols = slice(c * tf, (c + 1) * tf)
        gc = cb_ref[:, cols]
        for k in range(FFN_CONV):
            back = FFN_CONV - 1 - k
            tap = g_s[halo - back + lo:halo - back + hi, :] * cw_ref[k:k + 1, cols]
            if group and back:
                row = lo + lax.broadcasted_iota(jnp.int32, (hi - lo, 1), 0)
                tap = jnp.where(row % group >= back, tap, 0.0)
            gc = gc + tap
        return (_gelu(gc) * v_s[lo:hi, :]).astype(BF16)

    for r in range(n_sub):
        up_rows(0, r * rs, (r + 1) * rs)

    for c in range(n_chunks - 1):
        cols = slice(c * tf, (c + 1) * tf)
        up_rows(c + 1, 0, tm)
        part = _dot(gate_rows(c, 0, tm), wd_ref[cols, :])
        if c == 0:
            acc_s[...] = part
        else:
            acc_s[...] += part

    c = n_chunks - 1
    cols = slice(c * tf, (c + 1) * tf)
    for r in range(n_sub):
        rows = slice(r * rs, (r + 1) * rs)
        f = acc_s[rows, :] + _dot(gate_rows(c, r * rs, (r + 1) * rs), wd_ref[cols, :])
        out_ref[rows, :] = hmid[rows, :] + _rms(f, NORM_EPS) * gpost_ref[...]


def _mix_ffn(h, front, yr, ya, layer, wo, gmix, gpre, wu, cw, cb, wd, gpost,
             tm, tf, n_sub, n_pro, meta_rows):
    B, S, _ = h.shape
    halo = BF16_ROWS
    assert D_FF // tf >= 2 and tm % (n_sub * halo) == 0
    once = pl.Buffered(1)
    const = lambda shape: pl.BlockSpec((None,) + shape, lambda b, i: (layer, 0, 0),
                                       pipeline_mode=once)
    vec = _full((1, D_MODEL))
    if meta_rows:
        assert tm == B * N_META == FRONT
        stack = lambda a: a[:, PAD:FRONT].reshape(tm, a.shape[-1])
        h = front = stack(front)
        yr, ya = stack(yr), stack(ya)
        win, grid, out_rows = tm, (1, 1), tm
        rows = lambda width: _full((tm, width))
        h_spec = front_spec = _full((tm, D_MODEL))
        out_spec = _full((tm, D_MODEL))
        out_shape = jax.ShapeDtypeStruct((tm, D_MODEL), F32)
    else:
        win, grid = tm + halo, (B, S // tm)
        rows = lambda width: _frames_window(
            win, width, halo, lambda b, i: (b, FRONT - halo + i * tm))
        h_spec = _frames_window(win, D_MODEL, SUBLANES,
                                lambda b, i: (b, jnp.maximum(i * tm - halo, 0)))
        front_spec = pl.BlockSpec((None, FRONT, D_MODEL), lambda b, i: (b, 0, 0))
        out_spec = pl.BlockSpec((None, tm, D_MODEL), lambda b, i: (b, i, 0))
        out_shape = jax.ShapeDtypeStruct((B, S, D_MODEL), F32)
    assert win % (n_pro * halo) == 0
    out = pl.pallas_call(
        functools.partial(_mix_ffn_kernel, tf=tf, n_sub=n_sub, n_pro=n_pro,
                          group=N_META if meta_rows else None),
        grid=grid,
        in_specs=[h_spec, front_spec,
                  rows(LRU_WIDTH), rows(ATT_WIDTH), const((D_MODEL, D_MODEL)),
                  vec, vec, const((D_MODEL, 2 * D_FF)),
                  _full((FFN_CONV, D_FF)), _full((1, D_FF)), const((D_FF, D_MODEL)), vec],
        out_specs=out_spec,
        out_shape=out_shape,
        scratch_shapes=[pltpu.VMEM((tm + halo, D_MODEL), BF16),
                        pltpu.VMEM((tm, D_MODEL), F32),
                        pltpu.VMEM((tm + halo, tf), F32),
                        pltpu.VMEM((tm + halo, tf), F32),
                        pltpu.VMEM((tm, tf), F32), pltpu.VMEM((tm, tf), F32),
                        pltpu.VMEM((tm, D_MODEL), F32)],
        compiler_params=_params("parallel", "arbitrary"),
        name="mix_ffn_meta" if meta_rows else "mix_ffn",
    )(h, front, yr, ya, wo, gmix, gpre, wu, cw, cb, wd, gpost)
    if meta_rows:
        out = jnp.concatenate([jnp.zeros((B, PAD, D_MODEL), F32),
                               out.reshape(B, N_META, D_MODEL)], axis=1)
    return out


def _block_diag(w):
    per = GATE_TILE // (LRU_WIDTH // LRU_BLOCKS)
    bd = w.shape[-1]
    w = w.reshape(LRU_WIDTH // GATE_TILE, per, bd, bd)
    eye = jnp.eye(per, dtype=w.dtype)
    out = jnp.einsum('jpcd,pq->jpcqd', w, eye)
    return out.reshape(LRU_WIDTH // GATE_TILE, GATE_TILE, GATE_TILE).astype(BF16)


def _keep_rows(tt):
    p = jnp.arange(tt)
    orig = (p % SUBLANES) * (tt // SUBLANES) + p // SUBLANES
    keep = jnp.stack([(orig >= PAD).astype(F32), jnp.ones(tt, F32)])
    return jnp.broadcast_to(keep[:, :, None], (2, tt, LANES))


def _rope_tables(seq):
    inv = 1.0 / (ROPE_THETA ** (jnp.arange(0, HEAD_DIM, 2, dtype=F32) / HEAD_DIM))
    pos = jnp.concatenate([jnp.zeros(PAD, jnp.int32), jnp.arange(N_META + seq)]).astype(F32)
    ang = pos[:, None] * inv[None, :]
    cos, sin = jnp.cos(ang), jnp.sin(ang)
    reps = LANES // HEAD_DIM
    return (jnp.tile(jnp.concatenate([cos, cos], axis=1), (1, reps)),
            jnp.tile(jnp.concatenate([-sin, sin], axis=1), (1, reps)))


def kernel(x, meta_tokens, ln_mix_pre, ln_mix_post, ln_ffn_pre, ln_ffn_post, w_in, lru_conv_w, lru_conv_b, lru_wa, lru_ba, lru_wx, lru_bx, lru_a_param, lru_out_norm, lam_q1, lam_k1, lam_q2, lam_k2, diff_subln, w_out, w_up, ffn_conv_w, ffn_conv_b, w_down):
    B, S, _ = x.shape
    depth = w_in.shape[0]
    TP = FRONT + S
    assert TP % 528 == 0 and S % 512 == 0
    front = jnp.concatenate([jnp.zeros((PAD, D_MODEL), x.dtype), meta_tokens.astype(x.dtype)])
    front = jnp.broadcast_to(front[None], (B, FRONT, D_MODEL))
    h = x
    cos, sin = _rope_tables(S)
    vec = lambda a: a.reshape(1, -1)

    w_in, w_out, w_up, w_down = (w.astype(BF16) for w in (w_in, w_out, w_up, w_down))
    for l in range(depth):
        lambda_init = 0.8 - 0.6 * math.exp(-0.3 * l)
        q, k, v, yr = _in_lru(h, front, vec(ln_mix_pre[l]), l, w_in, cos, sin, _keep_rows(528),
                              lru_conv_w[l], vec(lru_conv_b[l]),
                              _block_diag(lru_wa[l]), vec(lru_ba[l]), _block_diag(lru_wx[l]),
                              vec(lru_bx[l]), vec(lru_a_param[l]), vec(lru_out_norm[l]),
                              tm=528, n_sub=3)
        lam = jnp.stack([lam_q1[l], lam_k1[l], lam_q2[l], lam_k2[l]])
        ya = _attn(q, k, v, lam, vec(diff_subln[l]), lambda_init, tq=512)
        ffn = functools.partial(
            _mix_ffn, h, front, yr, ya, l, w_out, vec(ln_mix_post[l]), vec(ln_ffn_pre[l]),
            w_up, ffn_conv_w[l], vec(ffn_conv_b[l]), w_down, vec(ln_ffn_post[l]), tf=1024)
        if l + 1 < depth:
            next_front = ffn(tm=FRONT, n_sub=2, n_pro=2, meta_rows=True)
        h = ffn(tm=512, n_sub=4, n_pro=3, meta_rows=False)
        if l + 1 < depth:
            front = next_front
    return h
```

```python
import functools
import math

import jax
import jax.numpy as jnp
from jax import lax
from jax.experimental import pallas as pl
from jax.experimental.pallas import tpu as pltpu

D_MODEL = 1024
N_META = 16
CHUNK = 64
LRU_WIDTH = 512
LRU_BLOCKS = 8
LRU_CONV = 4
LRU_C = 8.0
ATT_WIDTH = 512
HEADS = 4
HEAD_DIM = 64
V_DIM = 128
ROPE_THETA = 10000.0
D_FF = 3 * D_MODEL
FFN_CONV = 3
NORM_EPS = 1e-6
SUBLN_EPS = 1e-5
IN_WIDTH = 2 * LRU_WIDTH + 3 * ATT_WIDTH
NEG_INF = -1e30

LANES = 128
SUBLANES = 8
BF16_ROWS = 16
GATE_TILE = 256
FRONT = LANES
PAD = FRONT - N_META

ATTN_UNROLL = 2

VMEM_LIMIT = 56 * 1024 * 1024

F32 = jnp.float32
BF16 = jnp.bfloat16


def _dot(a, b):
    return jnp.dot(a, b, preferred_element_type=F32)


def _dot_t(a, b):
    return lax.dot_general(a, b, (((0,), (0,)), ((), ())), preferred_element_type=F32)


def _gelu(x):
    c = math.sqrt(2.0 / math.pi)
    return 0.5 * x * (1.0 + jnp.tanh(c * (x + 0.044715 * (x * x * x))))


def _sigmoid(x):
    return 1.0 / (1.0 + jnp.exp(-x))


def _rms(x, eps):
    return x * lax.rsqrt(jnp.mean(x * x, axis=-1, keepdims=True) + eps)


def _residual_rows(h_ref, front_ref, first, lo, hi, shift):
    direct = h_ref[lo:hi, :]
    base = FRONT - shift
    if hi <= shift:
        shifted = front_ref[base + lo:base + hi, :]
    elif lo >= shift:
        shifted = h_ref[lo - shift:hi - shift, :]
    else:
        shifted = jnp.concatenate([front_ref[base + lo:FRONT, :], h_ref[0:hi - shift, :]], axis=0)
    return jnp.where(first, shifted, direct)


def _frames_window(win, width, align, start):
    def index(*ids):
        b, row = start(*ids)
        return b, pl.multiple_of(row, align), 0

    return pl.BlockSpec((None, pl.Element(win), pl.Element(width)), index)


def _params(*sem):
    return pltpu.CompilerParams(dimension_semantics=sem, vmem_limit_bytes=VMEM_LIMIT)


def _full(shape):
    n = len(shape)
    return pl.BlockSpec(shape, lambda *_: (0,) * n)


def _in_lru_kernel(h_ref, front_ref, g_ref, w_ref, cos_ref, sin_ref, keep_ref, cw_ref, cb_ref,
                   wa_ref, ba_ref, wx_ref, bx_ref, ap_ref, gn_ref,
                   q_ref, k_ref, v_ref, yr_ref,
                   u_s, xr_s, gr_s, a_s, b_s, h_s, tails, carry, *, nt, n_sub):
    t = pl.program_id(0)
    tm = h_ref.shape[0]
    n_slabs = xr_s.shape[0]
    seg = tm // SUBLANES
    rs = tm // n_sub
    sub = lax.broadcasted_iota(jnp.int32, (SUBLANES, LRU_WIDTH), 0)
    fresh = ((t + nt - 1) % nt) == 0
    lead = (jnp.minimum(t, pl.num_programs(0) - 2) % nt) == 0

    @pl.when(t == 0)
    def _():
        xr_s[...] = jnp.zeros_like(xr_s)
        gr_s[...] = jnp.zeros_like(gr_s)
        tails[...] = jnp.zeros_like(tails)
        carry[...] = jnp.zeros_like(carry)

    def permuted(j):
        return jnp.concatenate([xr_s[c, pl.ds(j, SUBLANES, stride=seg), :]
                                for c in range(n_slabs)], axis=1)

    xp = [permuted(j) for j in range(seg)]
    taps = LRU_CONV - 1
    heads = []
    for k in range(taps):
        prev = jnp.where(fresh, 0.0, tails[k * SUBLANES:(k + 1) * SUBLANES, :])
        heads.append(jnp.where(sub == 0, pltpu.roll(prev, 1, 0),
                               pltpu.roll(xp[seg - taps + k], 1, 0)))
    for k in range(taps):
        tails[k * SUBLANES:(k + 1) * SUBLANES, :] = xp[seg - taps + k]
    ext = jnp.concatenate(heads + xp, axis=0)
    xc = cb_ref[...]
    for k in range(LRU_CONV):
        xc = xc + ext[k * SUBLANES:k * SUBLANES + tm, :] * cw_ref[k:k + 1, :]
    xcb = xc.astype(BF16)

    for r in range(n_sub):
        rows = slice(r * rs, (r + 1) * rs)
        hr = _residual_rows(h_ref, front_ref, lead, r * rs, (r + 1) * rs, FRONT)
        u_s[rows, :] = (_rms(hr, NORM_EPS) * g_ref[...]).astype(BF16)
    o = 2 * LRU_WIDTH
    zq = jnp.concatenate([_dot(u_s[r * rs:(r + 1) * rs, :], w_ref[:, o:o + ATT_WIDTH])
                          for r in range(n_sub)], axis=0)
    zk = _dot(u_s[...], w_ref[:, o + ATT_WIDTH:o + 2 * ATT_WIDTH])
    v_ref[...] = _dot(u_s[...], w_ref[:, o + 2 * ATT_WIDTH:o + 3 * ATT_WIDTH]).astype(BF16)
    zx = _dot(u_s[...], w_ref[:, 0:LRU_WIDTH])
    for c in range(n_slabs):
        xr_s[c] = zx[:, c * LANES:(c + 1) * LANES]

    def gate(wg_ref, b_ref):
        parts = [_dot(xcb[:, j * GATE_TILE:(j + 1) * GATE_TILE], wg_ref[j])
                 for j in range(LRU_WIDTH // GATE_TILE)]
        return _sigmoid(jnp.concatenate(parts, axis=1) + b_ref[...])

    r_gate = gate(wa_ref, ba_ref)
    i_gate = gate(wx_ref, bx_ref)

    ap = ap_ref[...]
    log_sig = jnp.minimum(ap, 0.0) - jnp.log1p(jnp.exp(-jnp.abs(ap)))
    a = jnp.exp2(r_gate * (LRU_C * math.log2(math.e) * log_sig))
    a_s[...] = a
    d = 1.0 - a * a
    b = jnp.where(d > 0.0, d * lax.rsqrt(d), 0.0) * (i_gate * xc)
    keep = keep_ref[...]
    b_s[...] = b * jnp.concatenate([keep] * n_slabs, axis=1)

    vreg = lambda ref, j: ref[j * SUBLANES:(j + 1) * SUBLANES, :]

    h_end = jnp.zeros((SUBLANES, LRU_WIDTH), F32)
    a_prod = jnp.ones((SUBLANES, LRU_WIDTH), F32)
    for j in range(seg):
        aj = vreg(a_s, j)
        h_end = aj * h_end + vreg(b_s, j)
        a_prod = aj * a_prod

    c_in = jnp.where(fresh, 0.0, carry[...])
    enter = c_in
    for _ in range(SUBLANES - 1):
        enter = jnp.where(sub == 0, c_in, pltpu.roll(h_end + a_prod * enter, 1, 0))
    last = h_end + a_prod * enter
    carry[...] = jnp.broadcast_to(last[SUBLANES - 1:SUBLANES, :], (SUBLANES, LRU_WIDTH))

    hv = enter
    for j in range(seg):
        hv = vreg(a_s, j) * hv + vreg(b_s, j)
        for c in range(n_slabs):
            h_s[c, pl.ds(j, SUBLANES, stride=seg), :] = hv[:, c * LANES:(c + 1) * LANES]

    hn = jnp.concatenate([h_s[c] for c in range(n_slabs)], axis=1)
    y = hn * _gelu(gr_s[...])
    yr_ref[...] = (_rms(y, NORM_EPS) * gn_ref[...]).astype(BF16)

    gr_s[...] = _dot(u_s[...], w_ref[:, LRU_WIDTH:2 * LRU_WIDTH])

    cos = cos_ref[...]
    sin = sin_ref[...]
    lane = lax.broadcasted_iota(jnp.int32, (tm, LANES), 1)
    low_half = (lane & (HEAD_DIM // 2)) == 0

    def rope(x):
        up = pltpu.roll(x, LANES - HEAD_DIM // 2, 1)
        down = pltpu.roll(x, HEAD_DIM // 2, 1)
        return x * cos + jnp.where(low_half, up, down) * sin

    q_scale = HEAD_DIM ** -0.5 * math.log2(math.e)
    for hd in range(HEADS):
        sl = slice(hd * LANES, (hd + 1) * LANES)
        q_ref[:, sl] = (rope(zq[:, sl]) * q_scale).astype(BF16)
        k_ref[:, sl] = rope(zk[:, sl]).astype(BF16)


def _in_lru(h, front, g, layer, w, cos, sin, keep, cw, cb, wa, ba, wx, bx, ap, gn, tm, n_sub):
    B = h.shape[0]
    TP = h.shape[1] + FRONT
    nt = TP // tm
    last = B * nt - 1
    assert tm % (n_sub * BF16_ROWS) == 0
    proj = lambda t: jnp.minimum(t, last)
    lru = lambda t: jnp.maximum(t - 1, 0)
    rows = lambda width, tile: pl.BlockSpec(
        (None, tm, width), lambda t: (tile(t) // nt, tile(t) % nt, 0))
    out = lambda width: jax.ShapeDtypeStruct((B, TP, width), BF16)
    vec = _full((1, LRU_WIDTH))
    wspec = _full((LRU_WIDTH // GATE_TILE, GATE_TILE, GATE_TILE))
    n_slabs = LRU_WIDTH // LANES
    taps = LRU_CONV - 1
    return pl.pallas_call(
        functools.partial(_in_lru_kernel, nt=nt, n_sub=n_sub),
        grid=(B * nt + 1,),
        in_specs=[_frames_window(tm, D_MODEL, SUBLANES, lambda t: (
                      proj(t) // nt, jnp.maximum((proj(t) % nt) * tm - FRONT, 0))),
                  pl.BlockSpec((None, FRONT, D_MODEL), lambda t: (proj(t) // nt, 0, 0)),
                  _full((1, D_MODEL)),
                  pl.BlockSpec((None, D_MODEL, IN_WIDTH), lambda t: (layer, 0, 0),
                               pipeline_mode=pl.Buffered(1)),
                  pl.BlockSpec((tm, LANES), lambda t: (proj(t) % nt, 0)),
                  pl.BlockSpec((tm, LANES), lambda t: (proj(t) % nt, 0)),
                  pl.BlockSpec((None, tm, LANES), lambda t: (jnp.minimum(lru(t) % nt, 1), 0, 0)),
                  _full((LRU_CONV, LRU_WIDTH)), vec, wspec, vec, wspec, vec, vec, vec],
        out_specs=[rows(ATT_WIDTH, proj), rows(ATT_WIDTH, proj), rows(ATT_WIDTH, proj),
                   rows(LRU_WIDTH, lru)],
        out_shape=[out(ATT_WIDTH), out(ATT_WIDTH), out(ATT_WIDTH), out(LRU_WIDTH)],
        scratch_shapes=[pltpu.VMEM((tm, D_MODEL), BF16),
                        pltpu.VMEM((n_slabs, tm, LANES), F32),
                        pltpu.VMEM((tm, LRU_WIDTH), F32),
                        pltpu.VMEM((tm, LRU_WIDTH), F32),
                        pltpu.VMEM((tm, LRU_WIDTH), F32),
                        pltpu.VMEM((n_slabs, tm, LANES), F32),
                        pltpu.VMEM((taps * SUBLANES, LRU_WIDTH), F32),
                        pltpu.VMEM((SUBLANES, LRU_WIDTH), F32)],
        compiler_params=_params("arbitrary"),
        name="in_lru",
    )(h, front, g, w, cos, sin, keep, cw, cb, wa, ba, wx, bx, ap, gn)


def _attn_kernel(q_ref, k_ref, v_ref, lam_ref, sub_ref, out_ref,
                 qc_s, s0, s1, x0, x1, m_s, l_s, acc_s, *, lambda_init, tq):
    n_tiles = (q_ref.shape[0] - FRONT) // tq
    lp = lam_ref[...]
    lam = (jnp.exp(jnp.sum(lp[0:1] * lp[1:2], axis=1, keepdims=True))
           - jnp.exp(jnp.sum(lp[2:3] * lp[3:4], axis=1, keepdims=True)) + lambda_init)
    gain = sub_ref[...] * (1.0 - lambda_init)
    k_meta = k_ref[PAD:FRONT, :]
    v_meta = v_ref[PAD:FRONT, :]

    def qcat(q):
        qt = q.T
        dim = lax.broadcasted_iota(jnp.int32, qt.shape, 0)
        zero = jnp.zeros_like(qt)
        return jnp.concatenate([jnp.where(dim < HEAD_DIM, qt, zero),
                                jnp.where(dim >= HEAD_DIM, qt, zero)], axis=1)

    def meta_state(qc):
        s = _dot(k_meta, qc)
        m = jnp.max(s, axis=0, keepdims=True)
        p = jnp.exp2(s - m)
        return m, jnp.sum(p, axis=0, keepdims=True), _dot_t(v_meta, p.astype(BF16))

    def finish(acc, l, rows):
        n = acc.shape[1] // 2
        o = acc * (1.0 / l)
        o = o[:, :n] - lam * o[:, n:]
        o = o * lax.rsqrt(jnp.mean(o * o, axis=0, keepdims=True) + SUBLN_EPS)
        out_ref[rows, :] = (o.T * gain).astype(BF16)

    def scores(s_ref, mx_ref, j):
        start = pl.multiple_of(FRONT + j * tq, LANES)
        kb = jnp.concatenate([k_ref[pl.ds(start, tq), :], k_meta], axis=0)
        s = _dot(kb, qc_s[...])
        s_ref[...] = s
        mx_ref[...] = jnp.max(s[0:tq], axis=0, keepdims=True)

    def absorb(s_ref, mx_ref, j, diagonal):
        vb = v_ref[pl.ds(pl.multiple_of(FRONT + j * tq, LANES), tq), :]
        if diagonal:
            s = s_ref[...]
            row = lax.broadcasted_iota(jnp.int32, (tq + N_META, 1), 0)
            kc = jnp.where(row < tq, row // CHUNK, 0)
            qc_ = (lax.broadcasted_iota(jnp.int32, (1, 2 * tq), 1) % tq) // CHUNK
            s = jnp.where(kc <= qc_, s, NEG_INF)
            mx = jnp.max(s, axis=0, keepdims=True)
            vb = jnp.concatenate([vb, v_meta], axis=0)
        else:
            s = s_ref[0:tq, :]
            mx = mx_ref[...]
        m_old = m_s[...]
        m_new = jnp.maximum(m_old, mx)
        alpha = jnp.exp2(m_old - m_new)
        p = jnp.exp2(s - m_new)
        l_s[...] = alpha * l_s[...] + jnp.sum(p, axis=0, keepdims=True)
        m_s[...] = m_new
        acc_s[...] = alpha * acc_s[...] + _dot_t(vb, p.astype(BF16))

    m, l, acc = meta_state(qcat(q_ref[0:FRONT, :]))
    finish(acc, l, slice(0, FRONT))
    out_ref[0:PAD, :] = jnp.zeros((PAD, V_DIM), BF16)

    def load_queries(i):
        qc_s[...] = qcat(q_ref[pl.ds(pl.multiple_of(FRONT + i * tq, LANES), tq), :])

    load_queries(0)
    scores(s0, x0, 0)

    def tile(i, c):
        m_s[...] = jnp.full_like(m_s, NEG_INF)
        l_s[...] = jnp.zeros_like(l_s)
        acc_s[...] = jnp.zeros_like(acc_s)

        bufs = ((s0, x0), (s1, x1))

        def run(first_block, n_plain, diagonal_last):
            for k in range(n_plain):
                scores(*bufs[(k + 1) % 2], first_block + k + 1)
                absorb(*bufs[k % 2], first_block + k, False)
            if diagonal_last:
                absorb(*bufs[n_plain % 2], first_block + n_plain, True)

        def group(jj, c2):
            run(ATTN_UNROLL * jj, ATTN_UNROLL, False)
            return c2

        lax.fori_loop(0, i // ATTN_UNROLL, group, 0)
        done = ATTN_UNROLL * (i // ATTN_UNROLL)
        for rest in range(ATTN_UNROLL):
            @pl.when(i - done == rest)
            def _():
                run(done, rest, True)

        acc, l = acc_s[...], l_s[...]
        load_queries(jnp.minimum(i + 1, n_tiles - 1))
        scores(s0, x0, 0)
        finish(acc, l, pl.ds(pl.multiple_of(FRONT + i * tq, LANES), tq))
        return c

    lax.fori_loop(0, n_tiles, tile, 0)


def _attn(q, k, v, lam, sub, lambda_init, tq):
    B, TP, _ = k.shape
    kern = functools.partial(_attn_kernel, lambda_init=lambda_init, tq=tq)
    return pl.pallas_call(
        kern,
        grid=(B, HEADS),
        in_specs=[pl.BlockSpec((None, TP, V_DIM), lambda b, h: (b, 0, h)),
                  pl.BlockSpec((None, TP, V_DIM), lambda b, h: (b, 0, h)),
                  pl.BlockSpec((None, TP, V_DIM), lambda b, h: (b, 0, h)),
                  _full((4, HEAD_DIM)), _full((1, V_DIM))],
        out_specs=pl.BlockSpec((None, TP, V_DIM), lambda b, h: (b, 0, h)),
        out_shape=jax.ShapeDtypeStruct((B, TP, ATT_WIDTH), BF16),
        scratch_shapes=[pltpu.VMEM((V_DIM, 2 * tq), BF16),
                        pltpu.VMEM((tq + N_META, 2 * tq), F32),
                        pltpu.VMEM((tq + N_META, 2 * tq), F32),
                        pltpu.VMEM((1, 2 * tq), F32), pltpu.VMEM((1, 2 * tq), F32),
                        pltpu.VMEM((1, 2 * tq), F32), pltpu.VMEM((1, 2 * tq), F32),
                        pltpu.VMEM((V_DIM, 2 * tq), F32)],
        compiler_params=_params("parallel", "parallel"),
        name="diff_attn",
    )(q, k, v, lam, sub)


def _mix_ffn_kernel(h_ref, front_ref, yr_ref, ya_ref, wo_ref, gmix_ref, gpre_ref, wu_ref,
                    cw_ref, cb_ref, wd_ref, gpost_ref, out_ref,
                    uext, hmid, g_a, g_b, v_a, v_b, acc_s, *, tf, n_sub, n_pro, group):
    tm = out_ref.shape[0]
    win = h_ref.shape[0]
    halo = BF16_ROWS
    lead = win - tm
    rs = tm // n_sub
    ps = win // n_pro
    first = pl.program_id(1) == 0
    n_chunks = D_FF // tf
    bufs = ((g_a, v_a), (g_b, v_b))

    if lead == 0:
        uext[0:halo, :] = jnp.zeros((halo, D_MODEL), BF16)

    ys = [_dot(yr_ref[r * ps:(r + 1) * ps, :], wo_ref[0:LRU_WIDTH, :])
          + _dot(ya_ref[r * ps:(r + 1) * ps, :], wo_ref[LRU_WIDTH:, :]) for r in range(n_pro)]
    for r in range(n_pro):
        lo, hi = r * ps, (r + 1) * ps
        hm = (_residual_rows(h_ref, front_ref, first, lo, hi, lead if lead else FRONT)
              + _rms(ys[r], NORM_EPS) * gmix_ref[...])
        tlo = max(lo - lead, 0)
        hmid[tlo:hi - lead, :] = hm[tlo + lead - lo:, :]
        uext[halo - lead + lo:halo - lead + hi, :] = (
            _rms(hm, NORM_EPS) * gpre_ref[...]).astype(BF16)

    def up_rows(c, lo, hi):
        g_s, v_s = bufs[c % 2]
        cols = slice(c * tf, (c + 1) * tf)
        vcols = slice(D_FF + c * tf, D_FF + (c + 1) * tf)
        glo = 0 if lo == 0 else halo + lo
        g_s[glo:halo + hi, :] = _dot(uext[glo:halo + hi, :], wu_ref[:, cols])
        v_s[lo:hi, :] = _dot(uext[halo + lo:halo + hi, :], wu_ref[:, vcols])

    def gate_rows(c, lo, hi):
        g_s, v_s = bufs[c % 2]
        cols = slice(c * tf, (c + 1) * tf)
        gc = cb_ref[:, cols]
        for k in range(FFN_CONV):
            back = FFN_CONV - 1 - k
            tap = g_s[halo - back + lo:halo - back + hi, :] * cw_ref[k:k + 1, cols]
            if group and back:
                row = lo + lax.broadcasted_iota(jnp.int32, (hi - lo, 1), 0)
                tap = jnp.where(row % group >= back, tap, 0.0)
            gc = gc + tap
        return (_gelu(gc) * v_s[lo:hi, :]).astype(BF16)

    for r in range(n_sub):
        up_rows(0, r * rs, (r + 1) * rs)

    for c in range(n_chunks - 1):
        cols = slice(c * tf, (c + 1) * tf)
        up_rows(c + 1, 0, tm)
        part = _dot(gate_rows(c, 0, tm), wd_ref[cols, :])
        if c == 0:
            acc_s[...] = part
        else:
            acc_s[...] += part

    c = n_chunks - 1
    cols = slice(c * tf, (c + 1) * tf)
    for r in range(n_sub):
        rows = slice(r * rs, (r + 1) * rs)
        f = acc_s[rows, :] + _dot(gate_rows(c, r * rs, (r + 1) * rs), wd_ref[cols, :])
        out_ref[rows, :] = hmid[rows, :] + _rms(f, NORM_EPS) * gpost_ref[...]


def _mix_ffn(h, front, yr, ya, layer, wo, gmix, gpre, wu, cw, cb, wd, gpost,
             tm, tf, n_sub, n_pro, meta_rows):
    B, S, _ = h.shape
    halo = BF16_ROWS
    assert D_FF // tf >= 2 and tm % (n_sub * halo) == 0
    once = pl.Buffered(1)
    const = lambda shape: pl.BlockSpec((None,) + shape, lambda b, i: (layer, 0, 0),
                                       pipeline_mode=once)
    vec = _full((1, D_MODEL))
    if meta_rows:
        assert tm == B * N_META == FRONT
        stack = lambda a: a[:, PAD:FRONT].reshape(tm, a.shape[-1])
        h = front = stack(front)
        yr, ya = stack(yr), stack(ya)
        win, grid, out_rows = tm, (1, 1), tm
        rows = lambda width: _full((tm, width))
        h_spec = front_spec = _full((tm, D_MODEL))
        out_spec = _full((tm, D_MODEL))
        out_shape = jax.ShapeDtypeStruct((tm, D_MODEL), F32)
    else:
        win, grid = tm + halo, (B, S // tm)
        rows = lambda width: _frames_window(
            win, width, halo, lambda b, i: (b, FRONT - halo + i * tm))
        h_spec = _frames_window(win, D_MODEL, SUBLANES,
                                lambda b, i: (b, jnp.maximum(i * tm - halo, 0)))
        front_spec = pl.BlockSpec((None, FRONT, D_MODEL), lambda b, i: (b, 0, 0))
        out_spec = pl.BlockSpec((None, tm, D_MODEL), lambda b, i: (b, i, 0))
        out_shape = jax.ShapeDtypeStruct((B, S, D_MODEL), F32)
    assert win % (n_pro * halo) == 0
    out = pl.pallas_call(
        functools.partial(_mix_ffn_kernel, tf=tf, n_sub=n_sub, n_pro=n_pro,
                          group=N_META if meta_rows else None),
        grid=grid,
        in_specs=[h_spec, front_spec,
                  rows(LRU_WIDTH), rows(ATT_WIDTH), const((D_MODEL, D_MODEL)),
                  vec, vec, const((D_MODEL, 2 * D_FF)),
                  _full((FFN_CONV, D_FF)), _full((1, D_FF)), const((D_FF, D_MODEL)), vec],
        out_specs=out_spec,
        out_shape=out_shape,
        scratch_shapes=[pltpu.VMEM((tm + halo, D_MODEL), BF16),
                        pltpu.VMEM((tm, D_MODEL), F32),
                        pltpu.VMEM((tm + halo, tf), F32),
                        pltpu.VMEM((tm + halo, tf), F32),
                        pltpu.VMEM((tm, tf), F32), pltpu.VMEM((tm, tf), F32),
                        pltpu.VMEM((tm, D_MODEL), F32)],
        compiler_params=_params("parallel", "arbitrary"),
        name="mix_ffn_meta" if meta_rows else "mix_ffn",
    )(h, front, yr, ya, wo, gmix, gpre, wu, cw, cb, wd, gpost)
    if meta_rows:
        out = jnp.concatenate([jnp.zeros((B, PAD, D_MODEL), F32),
                               out.reshape(B, N_META, D_MODEL)], axis=1)
    return out


def _block_diag(w):
    per = GATE_TILE // (LRU_WIDTH // LRU_BLOCKS)
    bd = w.shape[-1]
    w = w.reshape(LRU_WIDTH // GATE_TILE, per, bd, bd)
    eye = jnp.eye(per, dtype=w.dtype)
    out = jnp.einsum('jpcd,pq->jpcqd', w, eye)
    return out.reshape(LRU_WIDTH // GATE_TILE, GATE_TILE, GATE_TILE).astype(BF16)


def _keep_rows(tt):
    p = jnp.arange(tt)
    orig = (p % SUBLANES) * (tt // SUBLANES) + p // SUBLANES
    keep = jnp.stack([(orig >= PAD).astype(F32), jnp.ones(tt, F32)])
    return jnp.broadcast_to(keep[:, :, None], (2, tt, LANES))


def _rope_tables(seq):
    inv = 1.0 / (ROPE_THETA ** (jnp.arange(0, HEAD_DIM, 2, dtype=F32) / HEAD_DIM))
    pos = jnp.concatenate([jnp.zeros(PAD, jnp.int32), jnp.arange(N_META + seq)]).astype(F32)
    ang = pos[:, None] * inv[None, :]
    cos, sin = jnp.cos(ang), jnp.sin(ang)
    reps = LANES // HEAD_DIM
    return (jnp.tile(jnp.concatenate([cos, cos], axis=1), (1, reps)),
            jnp.tile(jnp.concatenate([-sin, sin], axis=1), (1, reps)))


def kernel(x, meta_tokens, ln_mix_pre, ln_mix_post, ln_ffn_pre, ln_ffn_post, w_in, lru_conv_w, lru_conv_b, lru_wa, lru_ba, lru_wx, lru_bx, lru_a_param, lru_out_norm, lam_q1, lam_k1, lam_q2, lam_k2, diff_subln, w_out, w_up, ffn_conv_w, ffn_conv_b, w_down):
    B, S, _ = x.shape
    depth = w_in.shape[0]
    TP = FRONT + S
    assert TP % 528 == 0 and S % 512 == 0
    front = jnp.concatenate([jnp.zeros((PAD, D_MODEL), x.dtype), meta_tokens.astype(x.dtype)])
    front = jnp.broadcast_to(front[None], (B, FRONT, D_MODEL))
    h = x
    cos, sin = _rope_tables(S)
    vec = lambda a: a.reshape(1, -1)

    w_in, w_out, w_up, w_down = (w.astype(BF16) for w in (w_in, w_out, w_up, w_down))
    for l in range(depth):
        lambda_init = 0.8 - 0.6 * math.exp(-0.3 * l)
        q, k, v, yr = _in_lru(h, front, vec(ln_mix_pre[l]), l, w_in, cos, sin, _keep_rows(528),
                              lru_conv_w[l], vec(lru_conv_b[l]),
                              _block_diag(lru_wa[l]), vec(lru_ba[l]), _block_diag(lru_wx[l]),
                              vec(lru_bx[l]), vec(lru_a_param[l]), vec(lru_out_norm[l]),
                              tm=528, n_sub=3)
        lam = jnp.stack([lam_q1[l], lam_k1[l], lam_q2[l], lam_k2[l]])
        ya = _attn(q, k, v, lam, vec(diff_subln[l]), lambda_init, tq=512)
        ffn = functools.partial(
            _mix_ffn, h, front, yr, ya, l, w_out, vec(ln_mix_post[l]), vec(ln_ffn_pre[l]),
            w_up, ffn_conv_w[l], vec(ffn_conv_b[l]), w_down, vec(ln_ffn_post[l]), tf=1536)
        if l + 1 < depth:
            next_front = ffn(tm=FRONT, n_sub=2, n_pro=2, meta_rows=True)
        h = ffn(tm=512, n_sub=4, n_pro=3, meta_rows=False)
        if l + 1 < depth:
            front = next_front
    return h
```

```python
import functools
import math

import jax
import jax.numpy as jnp
from jax import lax
from jax.experimental import pallas as pl
from jax.experimental.pallas import tpu as pltpu

D_MODEL = 1024
N_META = 16
CHUNK = 64
LRU_WIDTH = 512
LRU_BLOCKS = 8
LRU_CONV = 4
LRU_C = 8.0
ATT_WIDTH = 512
HEADS = 4
HEAD_DIM = 64
V_DIM = 128
ROPE_THETA = 10000.0
D_FF = 3 * D_MODEL
FFN_CONV = 3
NORM_EPS = 1e-6
SUBLN_EPS = 1e-5
IN_WIDTH = 2 * LRU_WIDTH + 3 * ATT_WIDTH
NEG_INF = -1e30

LANES = 128
SUBLANES = 8
BF16_ROWS = 16
GATE_TILE = 256
FRONT = LANES
PAD = FRONT - N_META

ATTN_UNROLL = 2

VMEM_LIMIT = 56 * 1024 * 1024

F32 = jnp.float32
BF16 = jnp.bfloat16


def _dot(a, b):
    return jnp.dot(a, b, preferred_element_type=F32)


def _dot_t(a, b):
    return lax.dot_general(a, b, (((0,), (0,)), ((), ())), preferred_element_type=F32)


def _gelu(x):
    c = math.sqrt(2.0 / math.pi)
    return 0.5 * x * (1.0 + jnp.tanh(c * (x + 0.044715 * (x * x * x))))


def _sigmoid(x):
    return 1.0 / (1.0 + jnp.exp(-x))


def _rms(x, eps):
    return x * lax.rsqrt(jnp.mean(x * x, axis=-1, keepdims=True) + eps)


def _residual_rows(h_ref, front_ref, first, lo, hi, shift):
    direct = h_ref[lo:hi, :]
    base = FRONT - shift
    if hi <= shift:
        shifted = front_ref[base + lo:base + hi, :]
    elif lo >= shift:
        shifted = h_ref[lo - shift:hi - shift, :]
    else:
        shifted = jnp.concatenate([front_ref[base + lo:FRONT, :], h_ref[0:hi - shift, :]], axis=0)
    return jnp.where(first, shifted, direct)


def _frames_window(win, width, align, start):
    def index(*ids):
        b, row = start(*ids)
        return b, pl.multiple_of(row, align), 0

    return pl.BlockSpec((None, pl.Element(win), pl.Element(width)), index)


def _params(*sem):
    return pltpu.CompilerParams(dimension_semantics=sem, vmem_limit_bytes=VMEM_LIMIT)


def _full(shape):
    n = len(shape)
    return pl.BlockSpec(shape, lambda *_: (0,) * n)


def _in_lru_kernel(h_ref, front_ref, g_ref, w_ref, cos_ref, sin_ref, keep_ref, cw_ref, cb_ref,
                   wa_ref, ba_ref, wx_ref, bx_ref, ap_ref, gn_ref,
                   q_ref, k_ref, v_ref, yr_ref,
                   u_s, xr_s, gr_s, a_s, b_s, h_s, tails, carry, *, nt, n_sub):
    t = pl.program_id(0)
    tm = h_ref.shape[0]
    n_slabs = xr_s.shape[0]
    seg = tm // SUBLANES
    rs = tm // n_sub
    sub = lax.broadcasted_iota(jnp.int32, (SUBLANES, LRU_WIDTH), 0)
    fresh = ((t + nt - 1) % nt) == 0
    lead = (jnp.minimum(t, pl.num_programs(0) - 2) % nt) == 0

    @pl.when(t == 0)
    def _():
        xr_s[...] = jnp.zeros_like(xr_s)
        gr_s[...] = jnp.zeros_like(gr_s)
        tails[...] = jnp.zeros_like(tails)
        carry[...] = jnp.zeros_like(carry)

    def permuted(j):
        return jnp.concatenate([xr_s[c, pl.ds(j, SUBLANES, stride=seg), :]
                                for c in range(n_slabs)], axis=1)

    xp = [permuted(j) for j in range(seg)]
    taps = LRU_CONV - 1
    heads = []
    for k in range(taps):
        prev = jnp.where(fresh, 0.0, tails[k * SUBLANES:(k + 1) * SUBLANES, :])
        heads.append(jnp.where(sub == 0, pltpu.roll(prev, 1, 0),
                               pltpu.roll(xp[seg - taps + k], 1, 0)))
    for k in range(taps):
        tails[k * SUBLANES:(k + 1) * SUBLANES, :] = xp[seg - taps + k]
    ext = jnp.concatenate(heads + xp, axis=0)
    xc = cb_ref[...]
    for k in range(LRU_CONV):
        xc = xc + ext[k * SUBLANES:k * SUBLANES + tm, :] * cw_ref[k:k + 1, :]
    xcb = xc.astype(BF16)

    for r in range(n_sub):
        rows = slice(r * rs, (r + 1) * rs)
        hr = _residual_rows(h_ref, front_ref, lead, r * rs, (r + 1) * rs, FRONT)
        u_s[rows, :] = (_rms(hr, NORM_EPS) * g_ref[...]).astype(BF16)
    o = 2 * LRU_WIDTH
    zq = jnp.concatenate([_dot(u_s[r * rs:(r + 1) * rs, :], w_ref[:, o:o + ATT_WIDTH])
                          for r in range(n_sub)], axis=0)
    zk = _dot(u_s[...], w_ref[:, o + ATT_WIDTH:o + 2 * ATT_WIDTH])
    v_ref[...] = _dot(u_s[...], w_ref[:, o + 2 * ATT_WIDTH:o + 3 * ATT_WIDTH]).astype(BF16)
    zx = _dot(u_s[...], w_ref[:, 0:LRU_WIDTH])
    for c in range(n_slabs):
        xr_s[c] = zx[:, c * LANES:(c + 1) * LANES]

    def gate(wg_ref, b_ref):
        parts = [_dot(xcb[:, j * GATE_TILE:(j + 1) * GATE_TILE], wg_ref[j])
                 for j in range(LRU_WIDTH // GATE_TILE)]
        return _sigmoid(jnp.concatenate(parts, axis=1) + b_ref[...])

    r_gate = gate(wa_ref, ba_ref)
    i_gate = gate(wx_ref, bx_ref)

    ap = ap_ref[...]
    log_sig = jnp.minimum(ap, 0.0) - jnp.log1p(jnp.exp(-jnp.abs(ap)))
    a = jnp.exp2(r_gate * (LRU_C * math.log2(math.e) * log_sig))
    a_s[...] = a
    d = 1.0 - a * a
    b = jnp.where(d > 0.0, d * lax.rsqrt(d), 0.0) * (i_gate * xc)
    keep = keep_ref[...]
    b_s[...] = b * jnp.concatenate([keep] * n_slabs, axis=1)

    vreg = lambda ref, j: ref[j * SUBLANES:(j + 1) * SUBLANES, :]

    h_end = jnp.zeros((SUBLANES, LRU_WIDTH), F32)
    a_prod = jnp.ones((SUBLANES, LRU_WIDTH), F32)
    for j in range(seg):
        aj = vreg(a_s, j)
        h_end = aj * h_end + vreg(b_s, j)
        a_prod = aj * a_prod

    c_in = jnp.where(fresh, 0.0, carry[...])
    enter = c_in
    for _ in range(SUBLANES - 1):
        enter = jnp.where(sub == 0, c_in, pltpu.roll(h_end + a_prod * enter, 1, 0))
    last = h_end + a_prod * enter
    carry[...] = jnp.broadcast_to(last[SUBLANES - 1:SUBLANES, :], (SUBLANES, LRU_WIDTH))

    hv = enter
    for j in range(seg):
        hv = vreg(a_s, j) * hv + vreg(b_s, j)
        for c in range(n_slabs):
            h_s[c, pl.ds(j, SUBLANES, stride=seg), :] = hv[:, c * LANES:(c + 1) * LANES]

    hn = jnp.concatenate([h_s[c] for c in range(n_slabs)], axis=1)
    y = hn * _gelu(gr_s[...])
    yr_ref[...] = (_rms(y, NORM_EPS) * gn_ref[...]).astype(BF16)

    gr_s[...] = _dot(u_s[...], w_ref[:, LRU_WIDTH:2 * LRU_WIDTH])

    cos = cos_ref[...]
    sin = sin_ref[...]
    lane = lax.broadcasted_iota(jnp.int32, (tm, LANES), 1)
    low_half = (lane & (HEAD_DIM // 2)) == 0

    def rope(x):
        up = pltpu.roll(x, LANES - HEAD_DIM // 2, 1)
        down = pltpu.roll(x, HEAD_DIM // 2, 1)
        return x * cos + jnp.where(low_half, up, down) * sin

    q_scale = HEAD_DIM ** -0.5 * math.log2(math.e)
    for hd in range(HEADS):
        sl = slice(hd * LANES, (hd + 1) * LANES)
        q_ref[:, sl] = (rope(zq[:, sl]) * q_scale).astype(BF16)
        k_ref[:, sl] = rope(zk[:, sl]).astype(BF16)


def _in_lru(h, front, g, layer, w, cos, sin, keep, cw, cb, wa, ba, wx, bx, ap, gn, tm, n_sub):
    B = h.shape[0]
    TP = h.shape[1] + FRONT
    nt = TP // tm
    last = B * nt - 1
    assert tm % (n_sub * BF16_ROWS) == 0
    proj = lambda t: jnp.minimum(t, last)
    lru = lambda t: jnp.maximum(t - 1, 0)
    rows = lambda width, tile: pl.BlockSpec(
        (None, tm, width), lambda t: (tile(t) // nt, tile(t) % nt, 0))
    out = lambda width: jax.ShapeDtypeStruct((B, TP, width), BF16)
    vec = _full((1, LRU_WIDTH))
    wspec = _full((LRU_WIDTH // GATE_TILE, GATE_TILE, GATE_TILE))
    n_slabs = LRU_WIDTH // LANES
    taps = LRU_CONV - 1
    return pl.pallas_call(
        functools.partial(_in_lru_kernel, nt=nt, n_sub=n_sub),
        grid=(B * nt + 1,),
        in_specs=[_frames_window(tm, D_MODEL, SUBLANES, lambda t: (
                      proj(t) // nt, jnp.maximum((proj(t) % nt) * tm - FRONT, 0))),
                  pl.BlockSpec((None, FRONT, D_MODEL), lambda t: (proj(t) // nt, 0, 0)),
                  _full((1, D_MODEL)),
                  pl.BlockSpec((None, D_MODEL, IN_WIDTH), lambda t: (layer, 0, 0),
                               pipeline_mode=pl.Buffered(1)),
                  pl.BlockSpec((tm, LANES), lambda t: (proj(t) % nt, 0)),
                  pl.BlockSpec((tm, LANES), lambda t: (proj(t) % nt, 0)),
                  pl.BlockSpec((None, tm, LANES), lambda t: (jnp.minimum(lru(t) % nt, 1), 0, 0)),
                  _full((LRU_CONV, LRU_WIDTH)), vec, wspec, vec, wspec, vec, vec, vec],
        out_specs=[rows(ATT_WIDTH, proj), rows(ATT_WIDTH, proj), rows(ATT_WIDTH, proj),
                   rows(LRU_WIDTH, lru)],
        out_shape=[out(ATT_WIDTH), out(ATT_WIDTH), out(ATT_WIDTH), out(LRU_WIDTH)],
        scratch_shapes=[pltpu.VMEM((tm, D_MODEL), BF16),
                        pltpu.VMEM((n_slabs, tm, LANES), F32),
                        pltpu.VMEM((tm, LRU_WIDTH), F32),
                        pltpu.VMEM((tm, LRU_WIDTH), F32),
                        pltpu.VMEM((tm, LRU_WIDTH), F32),
                        pltpu.VMEM((n_slabs, tm, LANES), F32),
                        pltpu.VMEM((taps * SUBLANES, LRU_WIDTH), F32),
                        pltpu.VMEM((SUBLANES, LRU_WIDTH), F32)],
        compiler_params=_params("arbitrary"),
        name="in_lru",
    )(h, front, g, w, cos, sin, keep, cw, cb, wa, ba, wx, bx, ap, gn)


def _attn_kernel(q_ref, k_ref, v_ref, lam_ref, sub_ref, out_ref,
                 qc_s, s0, s1, x0, x1, m_s, l_s, acc_s, *, lambda_init, tq):
    n_tiles = (q_ref.shape[0] - FRONT) // tq
    lp = lam_ref[...]
    lam = (jnp.exp(jnp.sum(lp[0:1] * lp[1:2], axis=1, keepdims=True))
           - jnp.exp(jnp.sum(lp[2:3] * lp[3:4], axis=1, keepdims=True)) + lambda_init)
    gain = sub_ref[...] * (1.0 - lambda_init)
    k_meta = k_ref[PAD:FRONT, :]
    v_meta = v_ref[PAD:FRONT, :]

    def qcat(q):
        qt = q.T
        dim = lax.broadcasted_iota(jnp.int32, qt.shape, 0)
        zero = jnp.zeros_like(qt)
        return jnp.concatenate([jnp.where(dim < HEAD_DIM, qt, zero),
                                jnp.where(dim >= HEAD_DIM, qt, zero)], axis=1)

    def meta_state(qc):
        s = _dot(k_meta, qc)
        m = jnp.max(s, axis=0, keepdims=True)
        p = jnp.exp2(s - m)
        return m, jnp.sum(p, axis=0, keepdims=True), _dot_t(v_meta, p.astype(BF16))

    def finish(acc, l, rows):
        n = acc.shape[1] // 2
        o = acc * (1.0 / l)
        o = o[:, :n] - lam * o[:, n:]
        o = o * lax.rsqrt(jnp.mean(o * o, axis=0, keepdims=True) + SUBLN_EPS)
        out_ref[rows, :] = (o.T * gain).astype(BF16)

    def scores(s_ref, mx_ref, j):
        start = pl.multiple_of(FRONT + j * tq, LANES)
        kb = jnp.concatenate([k_ref[pl.ds(start, tq), :], k_meta], axis=0)
        s = _dot(kb, qc_s[...])
        s_ref[...] = s
        mx_ref[...] = jnp.max(s[0:tq], axis=0, keepdims=True)

    def absorb(s_ref, mx_ref, j, diagonal):
        vb = v_ref[pl.ds(pl.multiple_of(FRONT + j * tq, LANES), tq), :]
        if diagonal:
            s = s_ref[...]
            row = lax.broadcasted_iota(jnp.int32, (tq + N_META, 1), 0)
            kc = jnp.where(row < tq, row // CHUNK, 0)
            qc_ = (lax.broadcasted_iota(jnp.int32, (1, 2 * tq), 1) % tq) // CHUNK
            s = jnp.where(kc <= qc_, s, NEG_INF)
            mx = jnp.max(s, axis=0, keepdims=True)
            vb = jnp.concatenate([vb, v_meta], axis=0)
        else:
            s = s_ref[0:tq, :]
            mx = mx_ref[...]
        m_old = m_s[...]
        m_new = jnp.maximum(m_old, mx)
        alpha = jnp.exp2(m_old - m_new)
        p = jnp.exp2(s - m_new)
        l_s[...] = alpha * l_s[...] + jnp.sum(p, axis=0, keepdims=True)
        m_s[...] = m_new
        acc_s[...] = alpha * acc_s[...] + _dot_t(vb, p.astype(BF16))

    m, l, acc = meta_state(qcat(q_ref[0:FRONT, :]))
    finish(acc, l, slice(0, FRONT))
    out_ref[0:PAD, :] = jnp.zeros((PAD, V_DIM), BF16)

    def load_queries(i):
        qc_s[...] = qcat(q_ref[pl.ds(pl.multiple_of(FRONT + i * tq, LANES), tq), :])

    load_queries(0)
    scores(s0, x0, 0)

    def tile(i, c):
        m_s[...] = jnp.full_like(m_s, NEG_INF)
        l_s[...] = jnp.zeros_like(l_s)
        acc_s[...] = jnp.zeros_like(acc_s)

        bufs = ((s0, x0), (s1, x1))

        def run(first_block, n_plain, diagonal_last):
            for k in range(n_plain):
                scores(*bufs[(k + 1) % 2], first_block + k + 1)
                absorb(*bufs[k % 2], first_block + k, False)
            if diagonal_last:
                absorb(*bufs[n_plain % 2], first_block + n_plain, True)

        def group(jj, c2):
            run(ATTN_UNROLL * jj, ATTN_UNROLL, False)
            return c2

        lax.fori_loop(0, i // ATTN_UNROLL, group, 0)
        done = ATTN_UNROLL * (i // ATTN_UNROLL)
        for rest in range(ATTN_UNROLL):
            @pl.when(i - done == rest)
            def _():
                run(done, rest, True)

        acc, l = acc_s[...], l_s[...]
        load_queries(jnp.minimum(i + 1, n_tiles - 1))
        scores(s0, x0, 0)
        finish(acc, l, pl.ds(pl.multiple_of(FRONT + i * tq, LANES), tq))
        return c

    lax.fori_loop(0, n_tiles, tile, 0)


def _attn(q, k, v, lam, sub, lambda_init, tq):
    B, TP, _ = k.shape
    kern = functools.partial(_attn_kernel, lambda_init=lambda_init, tq=tq)
    return pl.pallas_call(
        kern,
        grid=(B, HEADS),
        in_specs=[pl.BlockSpec((None, TP, V_DIM), lambda b, h: (b, 0, h)),
                  pl.BlockSpec((None, TP, V_DIM), lambda b, h: (b, 0, h)),
                  pl.BlockSpec((None, TP, V_DIM), lambda b, h: (b, 0, h)),
                  _full((4, HEAD_DIM)), _full((1, V_DIM))],
        out_specs=pl.BlockSpec((None, TP, V_DIM), lambda b, h: (b, 0, h)),
        out_shape=jax.ShapeDtypeStruct((B, TP, ATT_WIDTH), BF16),
        scratch_shapes=[pltpu.VMEM((V_DIM, 2 * tq), BF16),
                        pltpu.VMEM((tq + N_META, 2 * tq), F32),
                        pltpu.VMEM((tq + N_META, 2 * tq), F32),
                        pltpu.VMEM((1, 2 * tq), F32), pltpu.VMEM((1, 2 * tq), F32),
                        pltpu.VMEM((1, 2 * tq), F32), pltpu.VMEM((1, 2 * tq), F32),
                        pltpu.VMEM((V_DIM, 2 * tq), F32)],
        compiler_params=_params("parallel", "parallel"),
        name="diff_attn",
    )(q, k, v, lam, sub)


def _mix_ffn_kernel(h_ref, front_ref, yr_ref, ya_ref, wo_ref, gmix_ref, gpre_ref, wu_ref,
                    cw_ref, cb_ref, wd_ref, gpost_ref, out_ref,
                    uext, hmid, g_a, g_b, v_a, v_b, acc_s, *, tf, n_sub, n_pro, group):
    tm = out_ref.shape[0]
    win = h_ref.shape[0]
    halo = BF16_ROWS
    lead = win - tm
    rs = tm // n_sub
    ps = win // n_pro
    first = pl.program_id(1) == 0
    n_chunks = D_FF // tf
    bufs = ((g_a, v_a), (g_b, v_b))

    if lead == 0:
        uext[0:halo, :] = jnp.zeros((halo, D_MODEL), BF16)

    ys = [_dot(yr_ref[r * ps:(r + 1) * ps, :], wo_ref[0:LRU_WIDTH, :])
          + _dot(ya_ref[r * ps:(r + 1) * ps, :], wo_ref[LRU_WIDTH:, :]) for r in range(n_pro)]
    for r in range(n_pro):
        lo, hi = r * ps, (r + 1) * ps
        hm = (_residual_rows(h_ref, front_ref, first, lo, hi, lead if lead else FRONT)
              + _rms(ys[r], NORM_EPS) * gmix_ref[...])
        tlo = max(lo - lead, 0)
        hmid[tlo:hi - lead, :] = hm[tlo + lead - lo:, :]
        uext[halo - lead + lo:halo - lead + hi, :] = (
            _rms(hm, NORM_EPS) * gpre_ref[...]).astype(BF16)

    def up_rows(c, lo, hi):
        g_s, v_s = bufs[c % 2]
        cols = slice(c * tf, (c + 1) * tf)
        vcols = slice(D_FF + c * tf, D_FF + (c + 1) * tf)
        glo = 0 if lo == 0 else halo + lo
        g_s[glo:halo + hi, :] = _dot(uext[glo:halo + hi, :], wu_ref[:, cols])
        v_s[lo:hi, :] = _dot(uext[halo + lo:halo + hi, :], wu_ref[:, vcols])

    def gate_rows(c, lo, hi):
        g_s, v_s = bufs[c % 2]
        cols = slice(c * tf, (c + 1) * tf)
        gc = cb_ref[:, cols]
        for k in range(FFN_CONV):
            back = FFN_CONV - 1 - k
            tap = g_s[halo - back + lo:halo - back + hi, :] * cw_ref[k:k + 1, cols]
            if group and back:
                row = lo + lax.broadcasted_iota(jnp.int32, (hi - lo, 1), 0)
                tap = jnp.where(row % group >= back, tap, 0.0)
            gc = gc + tap
        return (_gelu(gc) * v_s[lo:hi, :]).astype(BF16)

    for r in range(n_sub):
        up_rows(0, r * rs, (r + 1) * rs)

    for c in range(n_chunks - 1):
        cols = slice(c * tf, (c + 1) * tf)
        up_rows(c + 1, 0, tm)
        part = _dot(gate_rows(c, 0, tm), wd_ref[cols, :])
        if c == 0:
            acc_s[...] = part
        else:
            acc_s[...] += part

    c = n_chunks - 1
    cols = slice(c * tf, (c + 1) * tf)
    for r in range(n_sub):
        rows = slice(r * rs, (r + 1) * rs)
        f = acc_s[rows, :] + _dot(gate_rows(c, r * rs, (r + 1) * rs), wd_ref[cols, :])
        out_ref[rows, :] = hmid[rows, :] + _rms(f, NORM_EPS) * gpost_ref[...]


def _mix_ffn(h, front, yr, ya, layer, wo, gmix, gpre, wu, cw, cb, wd, gpost,
             tm, tf, n_sub, n_pro, meta_rows):
    B, S, _ = h.shape
    halo = BF16_ROWS
    assert D_FF // tf >= 2 and tm % (n_sub * halo) == 0
    once = pl.Buffered(1)
    const = lambda shape: pl.BlockSpec((None,) + shape, lambda b, i: (layer, 0, 0),
                                       pipeline_mode=once)
    vec = _full((1, D_MODEL))
    if meta_rows:
        assert tm == B * N_META == FRONT
        stack = lambda a: a[:, PAD:FRONT].reshape(tm, a.shape[-1])
        h = front = stack(front)
        yr, ya = stack(yr), stack(ya)
        win, grid, out_rows = tm, (1, 1), tm
        rows = lambda width: _full((tm, width))
        h_spec = front_spec = _full((tm, D_MODEL))
        out_spec = _full((tm, D_MODEL))
        out_shape = jax.ShapeDtypeStruct((tm, D_MODEL), F32)
    else:
        win, grid = tm + halo, (B, S // tm)
        rows = lambda width: _frames_window(
            win, width, halo, lambda b, i: (b, FRONT - halo + i * tm))
        h_spec = _frames_window(win, D_MODEL, SUBLANES,
                                lambda b, i: (b, jnp.maximum(i * tm - halo, 0)))
        front_spec = pl.BlockSpec((None, FRONT, D_MODEL), lambda b, i: (b, 0, 0))
        out_spec = pl.BlockSpec((None, tm, D_MODEL), lambda b, i: (b, i, 0))
        out_shape = jax.ShapeDtypeStruct((B, S, D_MODEL), F32)
    assert win % (n_pro * halo) == 0
    out = pl.pallas_call(
        functools.partial(_mix_ffn_kernel, tf=tf, n_sub=n_sub, n_pro=n_pro,
                          group=N_META if meta_rows else None),
        grid=grid,
        in_specs=[h_spec, front_spec,
                  rows(LRU_WIDTH), rows(ATT_WIDTH), const((D_MODEL, D_MODEL)),
                  vec, vec, const((D_MODEL, 2 * D_FF)),
                  _full((FFN_CONV, D_FF)), _full((1, D_FF)), const((D_FF, D_MODEL)), vec],
        out_specs=out_spec,
        out_shape=out_shape,
        scratch_shapes=[pltpu.VMEM((tm + halo, D_MODEL), BF16),
                        pltpu.VMEM((tm, D_MODEL), F32),
                        pltpu.VMEM((tm + halo, tf), F32),
                        pltpu.VMEM((tm + halo, tf), F32),
                        pltpu.VMEM((tm, tf), F32), pltpu.VMEM((tm, tf), F32),
                        pltpu.VMEM((tm, D_MODEL), F32)],
        compiler_params=_params("parallel", "arbitrary"),
        name="mix_ffn_meta" if meta_rows else "mix_ffn",
    )(h, front, yr, ya, wo, gmix, gpre, wu, cw, cb, wd, gpost)
    if meta_rows:
        out = jnp.concatenate([jnp.zeros((B, PAD, D_MODEL), F32),
                               out.reshape(B, N_META, D_MODEL)], axis=1)
    return out


def _block_diag(w):
    per = GATE_TILE // (LRU_WIDTH // LRU_BLOCKS)
    bd = w.shape[-1]
    w = w.reshape(LRU_WIDTH // GATE_TILE, per, bd, bd)
    eye = jnp.eye(per, dtype=w.dtype)
    out = jnp.einsum('jpcd,pq->jpcqd', w, eye)
    return out.reshape(LRU_WIDTH // GATE_TILE, GATE_TILE, GATE_TILE).astype(BF16)


def _keep_rows(tt):
    p = jnp.arange(tt)
    orig = (p % SUBLANES) * (tt // SUBLANES) + p // SUBLANES
    keep = jnp.stack([(orig >= PAD).astype(F32), jnp.ones(tt, F32)])
    return jnp.broadcast_to(keep[:, :, None], (2, tt, LANES))


def _rope_tables(seq):
    inv = 1.0 / (ROPE_THETA ** (jnp.arange(0, HEAD_DIM, 2, dtype=F32) / HEAD_DIM))
    pos = jnp.concatenate([jnp.zeros(PAD, jnp.int32), jnp.arange(N_META + seq)]).astype(F32)
    ang = pos[:, None] * inv[None, :]
    cos, sin = jnp.cos(ang), jnp.sin(ang)
    reps = LANES // HEAD_DIM
    return (jnp.tile(jnp.concatenate([cos, cos], axis=1), (1, reps)),
            jnp.tile(jnp.concatenate([-sin, sin], axis=1), (1, reps)))


def kernel(x, meta_tokens, ln_mix_pre, ln_mix_post, ln_ffn_pre, ln_ffn_post, w_in, lru_conv_w, lru_conv_b, lru_wa, lru_ba, lru_wx, lru_bx, lru_a_param, lru_out_norm, lam_q1, lam_k1, lam_q2, lam_k2, diff_subln, w_out, w_up, ffn_conv_w, ffn_conv_b, w_down):
    B, S, _ = x.shape
    depth = w_in.shape[0]
    TP = FRONT + S
    assert TP % 528 == 0 and S % 512 == 0
    front = jnp.concatenate([jnp.zeros((PAD, D_MODEL), x.dtype), meta_tokens.astype(x.dtype)])
    front = jnp.broadcast_to(front[None], (B, FRONT, D_MODEL))
    h = x
    cos, sin = _rope_tables(S)
    vec = lambda a: a.reshape(1, -1)

    w_in, w_out, w_up, w_down = (w.astype(BF16) for w in (w_in, w_out, w_up, w_down))
    for l in range(depth):
        lambda_init = 0.8 - 0.6 * math.exp(-0.3 * l)
        q, k, v, yr = _in_lru(h, front, vec(ln_mix_pre[l]), l, w_in, cos, sin, _keep_rows(528),
                              lru_conv_w[l], vec(lru_conv_b[l]),
                              _block_diag(lru_wa[l]), vec(lru_ba[l]), _block_diag(lru_wx[l]),
                              vec(lru_bx[l]), vec(lru_a_param[l]), vec(lru_out_norm[l]),
                              tm=528, n_sub=3)
        lam = jnp.stack([lam_q1[l], lam_k1[l], lam_q2[l], lam_k2[l]])
        ya = _attn(q, k, v, lam, vec(diff_subln[l]), lambda_init, tq=512)
        ffn = functools.partial(
            _mix_ffn, h, front, yr, ya, l, w_out, vec(ln_mix_post[l]), vec(ln_ffn_pre[l]),
            w_up, ffn_conv_w[l], vec(ffn_conv_b[l]), w_down, vec(ln_ffn_post[l]), tf=1536)
        if l + 1 < depth:
            next_front = ffn(tm=FRONT, n_sub=2, n_pro=2, meta_rows=True)
        h = ffn(tm=512, n_sub=2, n_pro=3, meta_rows=False)
        if l + 1 < depth:
            front = next_front
    return h
```

```python
import functools
import math

import jax
import jax.numpy as jnp
from jax import lax
from jax.experimental import pallas as pl
from jax.experimental.pallas import tpu as pltpu

D_MODEL = 1024
N_META = 16
CHUNK = 64
LRU_WIDTH = 512
LRU_BLOCKS = 8
LRU_CONV = 4
LRU_C = 8.0
ATT_WIDTH = 512
HEADS = 4
HEAD_DIM = 64
V_DIM = 128
ROPE_THETA = 10000.0
D_FF = 3 * D_MODEL
FFN_CONV = 3
NORM_EPS = 1e-6
SUBLN_EPS = 1e-5
IN_WIDTH = 2 * LRU_WIDTH + 3 * ATT_WIDTH
NEG_INF = -1e30

LANES = 128
SUBLANES = 8
BF16_ROWS = 16
GATE_TILE = 256
FRONT = LANES
PAD = FRONT - N_META

ATTN_UNROLL = 4

VMEM_LIMIT = 56 * 1024 * 1024

F32 = jnp.float32
BF16 = jnp.bfloat16


def _dot(a, b):
    return jnp.dot(a, b, preferred_element_type=F32)


def _dot_t(a, b):
    return lax.dot_general(a, b, (((0,), (0,)), ((), ())), preferred_element_type=F32)


def _gelu(x):
    c = math.sqrt(2.0 / math.pi)
    return 0.5 * x * (1.0 + jnp.tanh(c * (x + 0.044715 * (x * x * x))))


def _sigmoid(x):
    return 1.0 / (1.0 + jnp.exp(-x))


def _rms(x, eps):
    return x * lax.rsqrt(jnp.mean(x * x, axis=-1, keepdims=True) + eps)


def _residual_rows(h_ref, front_ref, first, lo, hi, shift):
    direct = h_ref[lo:hi, :]
    base = FRONT - shift
    if hi <= shift:
        shifted = front_ref[base + lo:base + hi, :]
    elif lo >= shift:
        shifted = h_ref[lo - shift:hi - shift, :]
    else:
        shifted = jnp.concatenate([front_ref[base + lo:FRONT, :], h_ref[0:hi - shift, :]], axis=0)
    return jnp.where(first, shifted, direct)


def _frames_window(win, width, align, start):
    def index(*ids):
        b, row = start(*ids)
        return b, pl.multiple_of(row, align), 0

    return pl.BlockSpec((None, pl.Element(win), pl.Element(width)), index)


def _params(*sem):
    return pltpu.CompilerParams(dimension_semantics=sem, vmem_limit_bytes=VMEM_LIMIT)


def _full(shape):
    n = len(shape)
    return pl.BlockSpec(shape, lambda *_: (0,) * n)


def _in_lru_kernel(h_ref, front_ref, g_ref, w_ref, cos_ref, sin_ref, keep_ref, cw_ref, cb_ref,
                   wa_ref, ba_ref, wx_ref, bx_ref, ap_ref, gn_ref,
                   q_ref, k_ref, v_ref, yr_ref,
                   u_s, xr_s, gr_s, a_s, b_s, h_s, tails, carry, *, nt, n_sub):
    t = pl.program_id(0)
    tm = h_ref.shape[0]
    n_slabs = xr_s.shape[0]
    seg = tm // SUBLANES
    rs = tm // n_sub
    sub = lax.broadcasted_iota(jnp.int32, (SUBLANES, LRU_WIDTH), 0)
    fresh = ((t + nt - 1) % nt) == 0
    lead = (jnp.minimum(t, pl.num_programs(0) - 2) % nt) == 0

    @pl.when(t == 0)
    def _():
        xr_s[...] = jnp.zeros_like(xr_s)
        gr_s[...] = jnp.zeros_like(gr_s)
        tails[...] = jnp.zeros_like(tails)
        carry[...] = jnp.zeros_like(carry)

    def permuted(j):
        return jnp.concatenate([xr_s[c, pl.ds(j, SUBLANES, stride=seg), :]
                                for c in range(n_slabs)], axis=1)

    xp = [permuted(j) for j in range(seg)]
    taps = LRU_CONV - 1
    heads = []
    for k in range(taps):
        prev = jnp.where(fresh, 0.0, tails[k * SUBLANES:(k + 1) * SUBLANES, :])
        heads.append(jnp.where(sub == 0, pltpu.roll(prev, 1, 0),
                               pltpu.roll(xp[seg - taps + k], 1, 0)))
    for k in range(taps):
        tails[k * SUBLANES:(k + 1) * SUBLANES, :] = xp[seg - taps + k]
    ext = jnp.concatenate(heads + xp, axis=0)
    xc = cb_ref[...]
    for k in range(LRU_CONV):
        xc = xc + ext[k * SUBLANES:k * SUBLANES + tm, :] * cw_ref[k:k + 1, :]
    xcb = xc.astype(BF16)

    for r in range(n_sub):
        rows = slice(r * rs, (r + 1) * rs)
        hr = _residual_rows(h_ref, front_ref, lead, r * rs, (r + 1) * rs, FRONT)
        u_s[rows, :] = (_rms(hr, NORM_EPS) * g_ref[...]).astype(BF16)
    o = 2 * LRU_WIDTH
    zq = jnp.concatenate([_dot(u_s[r * rs:(r + 1) * rs, :], w_ref[:, o:o + ATT_WIDTH])
                          for r in range(n_sub)], axis=0)
    zk = _dot(u_s[...], w_ref[:, o + ATT_WIDTH:o + 2 * ATT_WIDTH])
    v_ref[...] = _dot(u_s[...], w_ref[:, o + 2 * ATT_WIDTH:o + 3 * ATT_WIDTH]).astype(BF16)
    zx = _dot(u_s[...], w_ref[:, 0:LRU_WIDTH])
    for c in range(n_slabs):
        xr_s[c] = zx[:, c * LANES:(c + 1) * LANES]

    def gate(wg_ref, b_ref):
        parts = [_dot(xcb[:, j * GATE_TILE:(j + 1) * GATE_TILE], wg_ref[j])
                 for j in range(LRU_WIDTH // GATE_TILE)]
        return _sigmoid(jnp.concatenate(parts, axis=1) + b_ref[...])

    r_gate = gate(wa_ref, ba_ref)
    i_gate = gate(wx_ref, bx_ref)

    ap = ap_ref[...]
    log_sig = jnp.minimum(ap, 0.0) - jnp.log1p(jnp.exp(-jnp.abs(ap)))
    a = jnp.exp2(r_gate * (LRU_C * math.log2(math.e) * log_sig))
    a_s[...] = a
    d = 1.0 - a * a
    b = jnp.where(d > 0.0, d * lax.rsqrt(d), 0.0) * (i_gate * xc)
    keep = keep_ref[...]
    b_s[...] = b * jnp.concatenate([keep] * n_slabs, axis=1)

    vreg = lambda ref, j: ref[j * SUBLANES:(j + 1) * SUBLANES, :]

    h_end = jnp.zeros((SUBLANES, LRU_WIDTH), F32)
    a_prod = jnp.ones((SUBLANES, LRU_WIDTH), F32)
    for j in range(seg):
        aj = vreg(a_s, j)
        h_end = aj * h_end + vreg(b_s, j)
        a_prod = aj * a_prod

    c_in = jnp.where(fresh, 0.0, carry[...])
    enter = c_in
    for _ in range(SUBLANES - 1):
        enter = jnp.where(sub == 0, c_in, pltpu.roll(h_end + a_prod * enter, 1, 0))
    last = h_end + a_prod * enter
    carry[...] = jnp.broadcast_to(last[SUBLANES - 1:SUBLANES, :], (SUBLANES, LRU_WIDTH))

    hv = enter
    for j in range(seg):
        hv = vreg(a_s, j) * hv + vreg(b_s, j)
        for c in range(n_slabs):
            h_s[c, pl.ds(j, SUBLANES, stride=seg), :] = hv[:, c * LANES:(c + 1) * LANES]

    hn = jnp.concatenate([h_s[c] for c in range(n_slabs)], axis=1)
    y = hn * _gelu(gr_s[...])
    yr_ref[...] = (_rms(y, NORM_EPS) * gn_ref[...]).astype(BF16)

    gr_s[...] = _dot(u_s[...], w_ref[:, LRU_WIDTH:2 * LRU_WIDTH])

    cos = cos_ref[...]
    sin = sin_ref[...]
    lane = lax.broadcasted_iota(jnp.int32, (tm, LANES), 1)
    low_half = (lane & (HEAD_DIM // 2)) == 0

    def rope(x):
        up = pltpu.roll(x, LANES - HEAD_DIM // 2, 1)
        down = pltpu.roll(x, HEAD_DIM // 2, 1)
        return x * cos + jnp.where(low_half, up, down) * sin

    q_scale = HEAD_DIM ** -0.5 * math.log2(math.e)
    for hd in range(HEADS):
        sl = slice(hd * LANES, (hd + 1) * LANES)
        q_ref[:, sl] = (rope(zq[:, sl]) * q_scale).astype(BF16)
        k_ref[:, sl] = rope(zk[:, sl]).astype(BF16)


def _in_lru(h, front, g, layer, w, cos, sin, keep, cw, cb, wa, ba, wx, bx, ap, gn, tm, n_sub):
    B = h.shape[0]
    TP = h.shape[1] + FRONT
    nt = TP // tm
    last = B * nt - 1
    assert tm % (n_sub * BF16_ROWS) == 0
    proj = lambda t: jnp.minimum(t, last)
    lru = lambda t: jnp.maximum(t - 1, 0)
    rows = lambda width, tile: pl.BlockSpec(
        (None, tm, width), lambda t: (tile(t) // nt, tile(t) % nt, 0))
    out = lambda width: jax.ShapeDtypeStruct((B, TP, width), BF16)
    vec = _full((1, LRU_WIDTH))
    wspec = _full((LRU_WIDTH // GATE_TILE, GATE_TILE, GATE_TILE))
    n_slabs = LRU_WIDTH // LANES
    taps = LRU_CONV - 1
    return pl.pallas_call(
        functools.partial(_in_lru_kernel, nt=nt, n_sub=n_sub),
        grid=(B * nt + 1,),
        in_specs=[_frames_window(tm, D_MODEL, SUBLANES, lambda t: (
                      proj(t) // nt, jnp.maximum((proj(t) % nt) * tm - FRONT, 0))),
                  pl.BlockSpec((None, FRONT, D_MODEL), lambda t: (proj(t) // nt, 0, 0)),
                  _full((1, D_MODEL)),
                  pl.BlockSpec((None, D_MODEL, IN_WIDTH), lambda t: (layer, 0, 0),
                               pipeline_mode=pl.Buffered(1)),
                  pl.BlockSpec((tm, LANES), lambda t: (proj(t) % nt, 0)),
                  pl.BlockSpec((tm, LANES), lambda t: (proj(t) % nt, 0)),
                  pl.BlockSpec((None, tm, LANES), lambda t: (jnp.minimum(lru(t) % nt, 1), 0, 0)),
                  _full((LRU_CONV, LRU_WIDTH)), vec, wspec, vec, wspec, vec, vec, vec],
        out_specs=[rows(ATT_WIDTH, proj), rows(ATT_WIDTH, proj), rows(ATT_WIDTH, proj),
                   rows(LRU_WIDTH, lru)],
        out_shape=[out(ATT_WIDTH), out(ATT_WIDTH), out(ATT_WIDTH), out(LRU_WIDTH)],
        scratch_shapes=[pltpu.VMEM((tm, D_MODEL), BF16),
                        pltpu.VMEM((n_slabs, tm, LANES), F32),
                        pltpu.VMEM((tm, LRU_WIDTH), F32),
                        pltpu.VMEM((tm, LRU_WIDTH), F32),
                        pltpu.VMEM((tm, LRU_WIDTH), F32),
                        pltpu.VMEM((n_slabs, tm, LANES), F32),
                        pltpu.VMEM((taps * SUBLANES, LRU_WIDTH), F32),
                        pltpu.VMEM((SUBLANES, LRU_WIDTH), F32)],
        compiler_params=_params("arbitrary"),
        name="in_lru",
    )(h, front, g, w, cos, sin, keep, cw, cb, wa, ba, wx, bx, ap, gn)


def _attn_kernel(q_ref, k_ref, v_ref, lam_ref, sub_ref, out_ref,
                 qc_s, s0, s1, x0, x1, m_s, l_s, acc_s, *, lambda_init, tq):
    n_tiles = (q_ref.shape[0] - FRONT) // tq
    lp = lam_ref[...]
    lam = (jnp.exp(jnp.sum(lp[0:1] * lp[1:2], axis=1, keepdims=True))
           - jnp.exp(jnp.sum(lp[2:3] * lp[3:4], axis=1, keepdims=True)) + lambda_init)
    gain = sub_ref[...] * (1.0 - lambda_init)
    k_meta = k_ref[PAD:FRONT, :]
    v_meta = v_ref[PAD:FRONT, :]

    def qcat(q):
        qt = q.T
        dim = lax.broadcasted_iota(jnp.int32, qt.shape, 0)
        zero = jnp.zeros_like(qt)
        return jnp.concatenate([jnp.where(dim < HEAD_DIM, qt, zero),
                                jnp.where(dim >= HEAD_DIM, qt, zero)], axis=1)

    def meta_state(qc):
        s = _dot(k_meta, qc)
        m = jnp.max(s, axis=0, keepdims=True)
        p = jnp.exp2(s - m)
        return m, jnp.sum(p, axis=0, keepdims=True), _dot_t(v_meta, p.astype(BF16))

    def finish(acc, l, rows):
        n = acc.shape[1] // 2
        o = acc * (1.0 / l)
        o = o[:, :n] - lam * o[:, n:]
        o = o * lax.rsqrt(jnp.mean(o * o, axis=0, keepdims=True) + SUBLN_EPS)
        out_ref[rows, :] = (o.T * gain).astype(BF16)

    def scores(s_ref, mx_ref, j):
        start = pl.multiple_of(FRONT + j * tq, LANES)
        kb = jnp.concatenate([k_ref[pl.ds(start, tq), :], k_meta], axis=0)
        s = _dot(kb, qc_s[...])
        s_ref[...] = s
        mx_ref[...] = jnp.max(s[0:tq], axis=0, keepdims=True)

    def absorb(s_ref, mx_ref, j, diagonal):
        vb = v_ref[pl.ds(pl.multiple_of(FRONT + j * tq, LANES), tq), :]
        if diagonal:
            s = s_ref[...]
            row = lax.broadcasted_iota(jnp.int32, (tq + N_META, 1), 0)
            kc = jnp.where(row < tq, row // CHUNK, 0)
            qc_ = (lax.broadcasted_iota(jnp.int32, (1, 2 * tq), 1) % tq) // CHUNK
            s = jnp.where(kc <= qc_, s, NEG_INF)
            mx = jnp.max(s, axis=0, keepdims=True)
            vb = jnp.concatenate([vb, v_meta], axis=0)
        else:
            s = s_ref[0:tq, :]
            mx = mx_ref[...]
        m_old = m_s[...]
        m_new = jnp.maximum(m_old, mx)
        alpha = jnp.exp2(m_old - m_new)
        p = jnp.exp2(s - m_new)
        l_s[...] = alpha * l_s[...] + jnp.sum(p, axis=0, keepdims=True)
        m_s[...] = m_new
        acc_s[...] = alpha * acc_s[...] + _dot_t(vb, p.astype(BF16))

    m, l, acc = meta_state(qcat(q_ref[0:FRONT, :]))
    finish(acc, l, slice(0, FRONT))
    out_ref[0:PAD, :] = jnp.zeros((PAD, V_DIM), BF16)

    def load_queries(i):
        qc_s[...] = qcat(q_ref[pl.ds(pl.multiple_of(FRONT + i * tq, LANES), tq), :])

    load_queries(0)
    scores(s0, x0, 0)

    def tile(i, c):
        m_s[...] = jnp.full_like(m_s, NEG_INF)
        l_s[...] = jnp.zeros_like(l_s)
        acc_s[...] = jnp.zeros_like(acc_s)

        bufs = ((s0, x0), (s1, x1))

        def run(first_block, n_plain, diagonal_last):
            for k in range(n_plain):
                scores(*bufs[(k + 1) % 2], first_block + k + 1)
                absorb(*bufs[k % 2], first_block + k, False)
            if diagonal_last:
                absorb(*bufs[n_plain % 2], first_block + n_plain, True)

        def group(jj, c2):
            run(ATTN_UNROLL * jj, ATTN_UNROLL, False)
            return c2

        lax.fori_loop(0, i // ATTN_UNROLL, group, 0)
        done = ATTN_UNROLL * (i // ATTN_UNROLL)
        for rest in range(ATTN_UNROLL):
            @pl.when(i - done == rest)
            def _():
                run(done, rest, True)

        acc, l = acc_s[...], l_s[...]
        load_queries(jnp.minimum(i + 1, n_tiles - 1))
        scores(s0, x0, 0)
        finish(acc, l, pl.ds(pl.multiple_of(FRONT + i * tq, LANES), tq))
        return c

    lax.fori_loop(0, n_tiles, tile, 0)


def _attn(q, k, v, lam, sub, lambda_init, tq):
    B, TP, _ = k.shape
    kern = functools.partial(_attn_kernel, lambda_init=lambda_init, tq=tq)
    return pl.pallas_call(
        kern,
        grid=(B, HEADS),
        in_specs=[pl.BlockSpec((None, TP, V_DIM), lambda b, h: (b, 0, h)),
                  pl.BlockSpec((None, TP, V_DIM), lambda b, h: (b, 0, h)),
                  pl.BlockSpec((None, TP, V_DIM), lambda b, h: (b, 0, h)),
                  _full((4, HEAD_DIM)), _full((1, V_DIM))],
        out_specs=pl.BlockSpec((None, TP, V_DIM), lambda b, h: (b, 0, h)),
        out_shape=jax.ShapeDtypeStruct((B, TP, ATT_WIDTH), BF16),
        scratch_shapes=[pltpu.VMEM((V_DIM, 2 * tq), BF16),
                        pltpu.VMEM((tq + N_META, 2 * tq), F32),
                        pltpu.VMEM((tq + N_META, 2 * tq), F32),
                        pltpu.VMEM((1, 2 * tq), F32), pltpu.VMEM((1, 2 * tq), F32),
                        pltpu.VMEM((1, 2 * tq), F32), pltpu.VMEM((1, 2 * tq), F32),
                        pltpu.VMEM((V_DIM, 2 * tq), F32)],
        compiler_params=_params("parallel", "parallel"),
        name="diff_attn",
    )(q, k, v, lam, sub)


def _mix_ffn_kernel(h_ref, front_ref, yr_ref, ya_ref, wo_ref, gmix_ref, gpre_ref, wu_ref,
                    cw_ref, cb_ref, wd_ref, gpost_ref, out_ref,
                    uext, hmid, g_a, g_b, v_a, v_b, acc_s, *, tf, n_sub, n_pro, group):
    tm = out_ref.shape[0]
    win = h_ref.shape[0]
    halo = BF16_ROWS
    lead = win - tm
    rs = tm // n_sub
    ps = win // n_pro
    first = pl.program_id(1) == 0
    n_chunks = D_FF // tf
    bufs = ((g_a, v_a), (g_b, v_b))

    if lead == 0:
        uext[0:halo, :] = jnp.zeros((halo, D_MODEL), BF16)

    ys = [_dot(yr_ref[r * ps:(r + 1) * ps, :], wo_ref[0:LRU_WIDTH, :])
          + _dot(ya_ref[r * ps:(r + 1) * ps, :], wo_ref[LRU_WIDTH:, :]) for r in range(n_pro)]
    for r in range(n_pro):
        lo, hi = r * ps, (r + 1) * ps
        hm = (_residual_rows(h_ref, front_ref, first, lo, hi, lead if lead else FRONT)
              + _rms(ys[r], NORM_EPS) * gmix_ref[...])
        tlo = max(lo - lead, 0)
        hmid[tlo:hi - lead, :] = hm[tlo + lead - lo:, :]
        uext[halo - lead + lo:halo - lead + hi, :] = (
            _rms(hm, NORM_EPS) * gpre_ref[...]).astype(BF16)

    def up_rows(c, lo, hi):
        g_s, v_s = bufs[c % 2]
        cols = slice(c * tf, (c + 1) * tf)
        vcols = slice(D_FF + c * tf, D_FF + (c + 1) * tf)
        glo = 0 if lo == 0 else halo + lo
        g_s[glo:halo + hi, :] = _dot(uext[glo:halo + hi, :], wu_ref[:, cols])
        v_s[lo:hi, :] = _dot(uext[halo + lo:halo + hi, :], wu_ref[:, vcols])

    def gate_rows(c, lo, hi):
        g_s, v_s = bufs[c % 2]
        cols = slice(c * tf, (c + 1) * tf)
        gc = cb_ref[:, cols]
        for k in range(FFN_CONV):
            back = FFN_CONV - 1 - k
            tap = g_s[halo - back + lo:halo - back + hi, :] * cw_ref[k:k + 1, cols]
            if group and back:
                row = lo + lax.broadcasted_iota(jnp.int32, (hi - lo, 1), 0)
                tap = jnp.where(row % group >= back, tap, 0.0)
            gc = gc + tap
        return (_gelu(gc) * v_s[lo:hi, :]).astype(BF16)

    for r in range(n_sub):
        up_rows(0, r * rs, (r + 1) * rs)

    for c in range(n_chunks - 1):
        cols = slice(c * tf, (c + 1) * tf)
        up_rows(c + 1, 0, tm)
        part = _dot(gate_rows(c, 0, tm), wd_ref[cols, :])
        if c == 0:
            acc_s[...] = part
        else:
            acc_s[...] += part

    c = n_chunks - 1
    cols = slice(c * tf, (c + 1) * tf)
    for r in range(n_sub):
        rows = slice(r * rs, (r + 1) * rs)
        f = acc_s[rows, :] + _dot(gate_rows(c, r * rs, (r + 1) * rs), wd_ref[cols, :])
        out_ref[rows, :] = hmid[rows, :] + _rms(f, NORM_EPS) * gpost_ref[...]


def _mix_ffn(h, front, yr, ya, layer, wo, gmix, gpre, wu, cw, cb, wd, gpost,
             tm, tf, n_sub, n_pro, meta_rows):
    B, S, _ = h.shape
    halo = BF16_ROWS
    assert D_FF // tf >= 2 and tm % (n_sub * halo) == 0
    once = pl.Buffered(1)
    const = lambda shape: pl.BlockSpec((None,) + shape, lambda b, i: (layer, 0, 0),
                                       pipeline_mode=once)
    vec = _full((1, D_MODEL))
    if meta_rows:
        assert tm == B * N_META == FRONT
        stack = lambda a: a[:, PAD:FRONT].reshape(tm, a.shape[-1])
        h = front = stack(front)
        yr, ya = stack(yr), stack(ya)
        win, grid, out_rows = tm, (1, 1), tm
        rows = lambda width: _full((tm, width))
        h_spec = front_spec = _full((tm, D_MODEL))
        out_spec = _full((tm, D_MODEL))
        out_shape = jax.ShapeDtypeStruct((tm, D_MODEL), F32)
    else:
        win, grid = tm + halo, (B, S // tm)
        rows = lambda width: _frames_window(
            win, width, halo, lambda b, i: (b, FRONT - halo + i * tm))
        h_spec = _frames_window(win, D_MODEL, SUBLANES,
                                lambda b, i: (b, jnp.maximum(i * tm - halo, 0)))
        front_spec = pl.BlockSpec((None, FRONT, D_MODEL), lambda b, i: (b, 0, 0))
        out_spec = pl.BlockSpec((None, tm, D_MODEL), lambda b, i: (b, i, 0))
        out_shape = jax.ShapeDtypeStruct((B, S, D_MODEL), F32)
    assert win % (n_pro * halo) == 0
    out = pl.pallas_call(
        functools.partial(_mix_ffn_kernel, tf=tf, n_sub=n_sub, n_pro=n_pro,
                          group=N_META if meta_rows else None),
        grid=grid,
        in_specs=[h_spec, front_spec,
                  rows(LRU_WIDTH), rows(ATT_WIDTH), const((D_MODEL, D_MODEL)),
                  vec, vec, const((D_MODEL, 2 * D_FF)),
                  _full((FFN_CONV, D_FF)), _full((1, D_FF)), const((D_FF, D_MODEL)), vec],
        out_specs=out_spec,
        out_shape=out_shape,
        scratch_shapes=[pltpu.VMEM((tm + halo, D_MODEL), BF16),
                        pltpu.VMEM((tm, D_MODEL), F32),
                        pltpu.VMEM((tm + halo, tf), F32),
                        pltpu.VMEM((tm + halo, tf), F32),
                        pltpu.VMEM((tm, tf), F32), pltpu.VMEM((tm, tf), F32),
                        pltpu.VMEM((tm, D_MODEL), F32)],
        compiler_params=_params("parallel", "arbitrary"),
        name="mix_ffn_meta" if meta_rows else "mix_ffn",
    )(h, front, yr, ya, wo, gmix, gpre, wu, cw, cb, wd, gpost)
    if meta_rows:
        out = jnp.concatenate([jnp.zeros((B, PAD, D_MODEL), F32),
                               out.reshape(B, N_META, D_MODEL)], axis=1)
    return out


def _block_diag(w):
    per = GATE_TILE // (LRU_WIDTH // LRU_BLOCKS)
    bd = w.shape[-1]
    w = w.reshape(LRU_WIDTH // GATE_TILE, per, bd, bd)
    eye = jnp.eye(per, dtype=w.dtype)
    out = jnp.einsum('jpcd,pq->jpcqd', w, eye)
    return out.reshape(LRU_WIDTH // GATE_TILE, GATE_TILE, GATE_TILE).astype(BF16)


def _keep_rows(tt):
    p = jnp.arange(tt)
    orig = (p % SUBLANES) * (tt // SUBLANES) + p // SUBLANES
    keep = jnp.stack([(orig >= PAD).astype(F32), jnp.ones(tt, F32)])
    return jnp.broadcast_to(keep[:, :, None], (2, tt, LANES))


def _rope_tables(seq):
    inv = 1.0 / (ROPE_THETA ** (jnp.arange(0, HEAD_DIM, 2, dtype=F32) / HEAD_DIM))
    pos = jnp.concatenate([jnp.zeros(PAD, jnp.int32), jnp.arange(N_META + seq)]).astype(F32)
    ang = pos[:, None] * inv[None, :]
    cos, sin = jnp.cos(ang), jnp.sin(ang)
    reps = LANES // HEAD_DIM
    return (jnp.tile(jnp.concatenate([cos, cos], axis=1), (1, reps)),
            jnp.tile(jnp.concatenate([-sin, sin], axis=1), (1, reps)))


def kernel(x, meta_tokens, ln_mix_pre, ln_mix_post, ln_ffn_pre, ln_ffn_post, w_in, lru_conv_w, lru_conv_b, lru_wa, lru_ba, lru_wx, lru_bx, lru_a_param, lru_out_norm, lam_q1, lam_k1, lam_q2, lam_k2, diff_subln, w_out, w_up, ffn_conv_w, ffn_conv_b, w_down):
    B, S, _ = x.shape
    depth = w_in.shape[0]
    TP = FRONT + S
    assert TP % 528 == 0 and S % 512 == 0
    front = jnp.concatenate([jnp.zeros((PAD, D_MODEL), x.dtype), meta_tokens.astype(x.dtype)])
    front = jnp.broadcast_to(front[None], (B, FRONT, D_MODEL))
    h = x
    cos, sin = _rope_tables(S)
    vec = lambda a: a.reshape(1, -1)

    w_in, w_out, w_up, w_down = (w.astype(BF16) for w in (w_in, w_out, w_up, w_down))
    for l in range(depth):
        lambda_init = 0.8 - 0.6 * math.exp(-0.3 * l)
        q, k, v, yr = _in_lru(h, front, vec(ln_mix_pre[l]), l, w_in, cos, sin, _keep_rows(528),
                              lru_conv_w[l], vec(lru_conv_b[l]),
                              _block_diag(lru_wa[l]), vec(lru_ba[l]), _block_diag(lru_wx[l]),
                              vec(lru_bx[l]), vec(lru_a_param[l]), vec(lru_out_norm[l]),
                              tm=528, n_sub=3)
        lam = jnp.stack([lam_q1[l], lam_k1[l], lam_q2[l], lam_k2[l]])
        ya = _attn(q, k, v, lam, vec(diff_subln[l]), lambda_init, tq=512)
        ffn = functools.partial(
            _mix_ffn, h, front, yr, ya, l, w_out, vec(ln_mix_post[l]), vec(ln_ffn_pre[l]),
            w_up, ffn_conv_w[l], vec(ffn_conv_b[l]), w_down, vec(ln_ffn_post[l]), tf=1536)
        if l + 1 < depth:
            next_front = ffn(tm=FRONT, n_sub=2, n_pro=2, meta_rows=True)
        h = ffn(tm=512, n_sub=2, n_pro=3, meta_rows=False)
        if l + 1 < depth:
            front = next_front
    return h
```
